```python
import jax, jax.numpy as jnp
from jax import lax
import numpy as np

D_MODEL = 1024
BATCH = 4
SEQ = 8192
DEPTH = 4

GRID_W = 64
CTX_LEN = 256
N_MIXERS = 2
RWKV_SLOT = 0
N_RWKV_LAYERS = (DEPTH + N_MIXERS - 1) // N_MIXERS
N_CONV_LAYERS = DEPTH // N_MIXERS
HEAD_SIZE = 64
N_HEADS = D_MODEL // HEAD_SIZE
DECAY_LORA = D_MODEL // 16
ICLR_LORA = D_MODEL // 16
GATE_LORA = D_MODEL // 8
GN_EPS = HEAD_SIZE * 1e-5
CONV_WIDTH = 3
N_EXPERTS = 32
TOP_K = 4
D_FF = D_MODEL
SWIGLU_LIMIT = 7.0
SWIGLU_ALPHA = 1.702
EXPERT_BLOCK = 256
RMS_EPS = 1e-6
N_MOD = 6

kernel_name = 'hybrid_rwkv7_shortconv_moe_dit'


def rmsnorm(x, g):
    xf = x.astype(jnp.float32)
    y = xf * lax.rsqrt(jnp.mean(xf * xf, axis=-1, keepdims=True) + RMS_EPS)
    return (y * g.astype(jnp.float32)).astype(x.dtype)


def ada_mods(cond, w, b):
    mods = jax.nn.silu(cond) @ w + b
    return jnp.split(mods[:, None, :], N_MOD, axis=-1)


def modulate(x, g, shift, scale):
    return rmsnorm(x, g) * (1 + scale) + shift


def row_neighbours(x, n_rows):
    b, l, ch = x.shape
    rows = jnp.pad(x.reshape(b, n_rows, l // n_rows, ch), ((0, 0), (0, 0), (1, 1), (0, 0)))
    return rows[:, :, :-2].reshape(b, l, ch), rows[:, :, 2:].reshape(b, l, ch)


def split_heads(t):
    return t.reshape(t.shape[:-1] + (N_HEADS, HEAD_SIZE))


def rwkv_streams(h, n_rows, mu, w_rkv, w0, w1, w2, a0, a1, a2, g1, g2, k_k, k_a, readout):
    f32 = jnp.float32
    x_prev, x_next = row_neighbours(h, n_rows)
    d_prev, d_next = x_prev - h, x_next - h

    def lerp(j):
        return h + mu[j, 0] * d_prev + mu[j, 1] * d_next

    k = (lerp(1) @ w_rkv[1]).astype(f32)
    v = split_heads((lerp(2) @ w_rkv[2]).astype(f32))
    xw, xa = lerp(3), lerp(4)
    w_log = w0[:, None, None, :] + jnp.einsum(
        'eblr,erd->ebld', jnp.tanh(jnp.einsum('bld,edr->eblr', xw, w1)), w2)
    decay = jnp.exp(-jnp.exp(-jax.nn.softplus(-w_log.astype(f32)) - 0.5))
    a = jax.nn.sigmoid((a0[:, None, None, :] + jnp.einsum(
        'eblr,erd->ebld', jnp.einsum('bld,edr->eblr', xa, a1), a2)).astype(f32))
    kk = split_heads(k * k_k)
    kk = kk / jnp.maximum(jnp.sqrt(jnp.sum(kk * kk, axis=-1, keepdims=True)), 1e-12)
    k_dir = split_heads(k * (1 + (a - 1) * k_a))
    s = {'decay': split_heads(decay), 'a': split_heads(a), 'k': k_dir, 'v': v, 'kk': kk}
    if readout:
        s['r'] = split_heads((lerp(0) @ w_rkv[0]).astype(f32))
        s['g'] = jax.nn.sigmoid(lerp(5) @ g1) @ g2
    return s


def wkv_scan(state0, s, d, reverse, readout):
    xs = [s['decay'][d], s['k'][d], s['v'], s['kk'], s['a'][d]] + ([s['r']] if readout else [])
    xs = tuple(jnp.moveaxis(t, 1, 0) for t in xs)

    def step(S, inp):
        w_t, k_t, v_t, kk_t, a_t = inp[:5]
        sa = jnp.einsum('bhvk,bhk->bhv', S, kk_t)
        S = (S * w_t[:, :, None, :] - sa[..., None] * (kk_t * a_t)[:, :, None, :]
             + v_t[..., None] * k_t[:, :, None, :])
        return S, (jnp.einsum('bhvk,bhk->bhv', S, inp[5]) if readout else None)

    s_last, ys = lax.scan(step, state0, xs, reverse=reverse)
    return s_last, (jnp.moveaxis(ys, 0, 1) if readout else None)


def rwkv_readout(y, s, ln_g, ln_b, r_k, w_o, out_dtype):
    b, l = y.shape[:2]
    mean = jnp.mean(y, axis=-1, keepdims=True)
    var = jnp.mean(jnp.square(y - mean), axis=-1, keepdims=True)
    yn = ((y - mean) * lax.rsqrt(var + GN_EPS)).reshape(b, l, D_MODEL) * ln_g + ln_b
    bonus = jnp.einsum('blhn,dblhn,hn->blh', s['r'], s['k'], r_k)[..., None] * s['v']
    z = (yn + bonus.reshape(b, l, D_MODEL)).astype(out_dtype) * s['g']
    return z @ w_o


def rwkv_mixer(h_lat, h_ctx, n_rows, mu, w_rkv, w0, w1, w2, a0, a1, a2, g1, g2, k_k, k_a,
               r_k, ln_g, ln_b, w_o, ctx_readout):
    s_ctx = rwkv_streams(h_ctx, 1, mu, w_rkv, w0, w1, w2, a0, a1, a2, g1, g2, k_k, k_a, ctx_readout)
    s_lat = rwkv_streams(h_lat, n_rows, mu, w_rkv, w0, w1, w2, a0, a1, a2, g1, g2, k_k, k_a, True)
    state0 = jnp.zeros((h_lat.shape[0], N_HEADS, HEAD_SIZE, HEAD_SIZE), jnp.float32)
    y_lat, y_ctx = 0.0, 0.0
    for d in range(2):
        rev = d == 1
        s_c, yc = wkv_scan(state0, s_ctx, d, rev, ctx_readout)
        _, yl = wkv_scan(s_c, s_lat, d, rev, True)
        y_lat = y_lat + yl
        if ctx_readout:
            y_ctx = y_ctx + yc
    out_lat = rwkv_readout(y_lat, s_lat, ln_g, ln_b, r_k, w_o, h_lat.dtype)
    out_ctx = rwkv_readout(y_ctx, s_ctx, ln_g, ln_b, r_k, w_o, h_ctx.dtype) if ctx_readout else None
    return out_lat, out_ctx


def short_conv_mixer(h, n_rows, w_in, conv_w, w_out):
    b_gate, c_gate, u = jnp.split(h @ w_in, 3, axis=-1)
    z = c_gate * u
    z_prev, z_next = row_neighbours(z, n_rows)
    z = conv_w[0] * z_prev + conv_w[1] * z + conv_w[2] * z_next
    return (b_gate * z) @ w_out


def moe_ffn(h, router_w, router_b, w_gu, b_gu, w_down, b_down):
    f32 = jnp.float32
    t, d = h.shape
    logits = h.astype(f32) @ router_w.astype(f32) + router_b.astype(f32)
    top_logit, top_e = lax.top_k(logits, TOP_K)
    gate = jax.nn.softmax(top_logit, axis=-1)
    expert = top_e.reshape(-1).astype(jnp.int32)
    token = jnp.repeat(jnp.arange(t, dtype=jnp.int32), TOP_K)
    n_assign = t * TOP_K
    counts = jnp.zeros((N_EXPERTS,), jnp.int32).at[expert].add(1)
    padded = (counts + EXPERT_BLOCK - 1) // EXPERT_BLOCK * EXPERT_BLOCK
    start = jnp.cumsum(counts) - counts
    pad_end = jnp.cumsum(padded)
    pad_start = pad_end - padded
    order = jnp.argsort(expert)
    e_sorted = expert[order]
    dest = pad_start[e_sorted] + jnp.arange(n_assign, dtype=jnp.int32) - start[e_sorted]
    n_blocks = -(-(n_assign + N_EXPERTS * (EXPERT_BLOCK - 1)) // EXPERT_BLOCK)
    n_rows = n_blocks * EXPERT_BLOCK
    row_tok = jnp.zeros((n_rows,), jnp.int32).at[dest].set(token[order])
    row_w = jnp.zeros((n_rows,), f32).at[dest].set(gate.reshape(-1)[order])
    block_start = jnp.arange(n_blocks, dtype=jnp.int32) * EXPERT_BLOCK
    block_e = jnp.minimum(jnp.searchsorted(pad_end, block_start, side='right'), N_EXPERTS - 1)
    xb = h[row_tok].reshape(n_blocks, EXPERT_BLOCK, d)

    def expert_block(args):
        xe, e = args
        g_, u_ = jnp.split(xe @ w_gu[e] + b_gu[e], 2, axis=-1)
        g_ = jnp.minimum(g_, SWIGLU_LIMIT)
        u_ = jnp.clip(u_, -SWIGLU_LIMIT, SWIGLU_LIMIT)
        return ((u_ + 1) * (g_ * jax.nn.sigmoid(SWIGLU_ALPHA * g_))) @ w_down[e] + b_down[e]

    yb = lax.map(expert_block, (xb, block_e))
    y = (yb.reshape(n_rows, d) * row_w[:, None]).astype(h.dtype)
    return jnp.zeros_like(h).at[row_tok].add(y)


def context_read_later(i):
    return any(j % N_MIXERS == RWKV_SLOT for j in range(i + 1, DEPTH))


def setup_inputs(seed: int = 0) -> dict:
    key = jax.random.key(seed)
    ks = iter(jax.random.split(key, 48))
    D, NA, NB, E, F = D_MODEL, N_RWKV_LAYERS, N_CONV_LAYERS, N_EXPERTS, D_FF

    def nrm(shape, scale):
        return jax.random.normal(next(ks), shape, jnp.float32) * scale

    chan = jnp.arange(D, dtype=jnp.float32) / (D - 1)
    return {
        'x': nrm((BATCH, SEQ, D), 1.0),
        'c': nrm((BATCH, D), 1.0),
        'ctx': nrm((BATCH, CTX_LEN, D), 1.0),
        'c_ctx': nrm((D,), 1.0),
        'ada_w': nrm((DEPTH, D, N_MOD * D), 0.5 * D ** -0.5),
        'ada_b': nrm((DEPTH, N_MOD * D), 0.01),
        'norm_g': 1.0 + nrm((DEPTH, 2, D), 0.05),
        'final_norm_g': 1.0 + nrm((D,), 0.05),
        'rw_mu': 0.5 * jax.random.uniform(next(ks), (NA, 6, 2, D), jnp.float32),
        'rw_w_rkv': nrm((NA, 3, D, D), D ** -0.5),
        'rw_w0': (-6.5 + 5.0 * chan ** 0.85)[None, None, :] + nrm((NA, 2, D), 0.1),
        'rw_w1': nrm((NA, 2, D, DECAY_LORA), D ** -0.5),
        'rw_w2': nrm((NA, 2, DECAY_LORA, D), 0.1 * DECAY_LORA ** -0.5),
        'rw_a0': nrm((NA, 2, D), 0.1),
        'rw_a1': nrm((NA, 2, D, ICLR_LORA), D ** -0.5),
        'rw_a2': nrm((NA, 2, ICLR_LORA, D), 0.1 * ICLR_LORA ** -0.5),
        'rw_g1': nrm((NA, D, GATE_LORA), D ** -0.5),
        'rw_g2': nrm((NA, GATE_LORA, D), GATE_LORA ** -0.5),
        'rw_k_k': 0.85 + nrm((NA, D), 0.02),
        'rw_k_a': 1.0 + nrm((NA, D), 0.02),
        'rw_r_k': nrm((NA, N_HEADS, HEAD_SIZE), 0.1),
        'rw_ln_g': 1.0 + nrm((NA, D), 0.05),
        'rw_ln_b': nrm((NA, D), 0.01),
        'rw_w_o': nrm((NA, D, D), D ** -0.5),
        'sc_w_in': nrm((NB, D, 3 * D), D ** -0.5),
        'sc_conv': nrm((NB, CONV_WIDTH, D), CONV_WIDTH ** -0.5),
        'sc_w_out': nrm((NB, D, D), D ** -0.5),
        'moe_router_w': nrm((DEPTH, D, E), D ** -0.5),
        'moe_router_b': nrm((DEPTH, E), 0.01),
        'moe_w_gu': nrm((DEPTH, E, D, 2 * F), D ** -0.5),
        'moe_b_gu': nrm((DEPTH, E, 2 * F), 0.01),
        'moe_w_down': nrm((DEPTH, E, F, D), F ** -0.5),
        'moe_b_down': nrm((DEPTH, E, D), 0.01),
    }


def reference(x, c, ctx, c_ctx, ada_w, ada_b, norm_g, final_norm_g, rw_mu, rw_w_rkv, rw_w0, rw_w1,
              rw_w2, rw_a0, rw_a1, rw_a2, rw_g1, rw_g2, rw_k_k, rw_k_a, rw_r_k, rw_ln_g, rw_ln_b,
              rw_w_o, sc_w_in, sc_conv, sc_w_out, moe_router_w, moe_router_b, moe_w_gu, moe_b_gu,
              moe_w_down, moe_b_down):
    b, seq, d = x.shape
    rows = seq // GRID_W
    xc = ctx
    for i in range(DEPTH):
        kind, j = i % N_MIXERS, i // N_MIXERS
        ctx_on = context_read_later(i)
        m = ada_mods(c, ada_w[i], ada_b[i])
        mc = ada_mods(c_ctx[None], ada_w[i], ada_b[i]) if (kind == RWKV_SLOT or ctx_on) else None
        h = modulate(x, norm_g[i, 0], m[0], m[1])
        if kind == RWKV_SLOT:
            hc = modulate(xc, norm_g[i, 0], mc[0], mc[1])
            y, yc = rwkv_mixer(h, hc, rows, rw_mu[j], rw_w_rkv[j], rw_w0[j], rw_w1[j], rw_w2[j],
                               rw_a0[j], rw_a1[j], rw_a2[j], rw_g1[j], rw_g2[j], rw_k_k[j],
                               rw_k_a[j], rw_r_k[j], rw_ln_g[j], rw_ln_b[j], rw_w_o[j], ctx_on)
        else:
            y = short_conv_mixer(h, rows, sc_w_in[j], sc_conv[j], sc_w_out[j])
            yc = (short_conv_mixer(modulate(xc, norm_g[i, 0], mc[0], mc[1]), 1, sc_w_in[j],
                                   sc_conv[j], sc_w_out[j]) if ctx_on else None)
        x = x + m[2] * y
        h = modulate(x, norm_g[i, 1], m[3], m[4]).reshape(-1, d)
        if ctx_on:
            xc = xc + mc[2] * yc
            hc = modulate(xc, norm_g[i, 1], mc[3], mc[4]).reshape(-1, d)
            f = moe_ffn(jnp.concatenate([h, hc], axis=0), moe_router_w[i], moe_router_b[i],
                        moe_w_gu[i], moe_b_gu[i], moe_w_down[i], moe_b_down[i])
            x = x + m[5] * f[:b * seq].reshape(b, seq, d)
            xc = xc + mc[5] * f[b * seq:].reshape(xc.shape)
        else:
            f = moe_ffn(h, moe_router_w[i], moe_router_b[i], moe_w_gu[i], moe_b_gu[i],
                        moe_w_down[i], moe_b_down[i])
            x = x + m[5] * f.reshape(b, seq, d)
    return rmsnorm(x, final_norm_g)
```

```python
import functools
import math

import jax
import jax.numpy as jnp
from jax import lax
from jax.experimental import pallas as pl
from jax.experimental.pallas import tpu as pltpu

F32 = jnp.float32
BF16 = jnp.bfloat16

HEAD = 64
CHUNK = 64
GROUP = 4
GL = GROUP * HEAD
LANES = 128
GRID_W = 64
TOKEN_TILE = 256
RMS_EPS = 1e-6
GN_EPS = HEAD * 1e-5
N_EXPERTS = 32
TOP_K = 4
N_MOD = 6
SWIGLU_LIMIT = 7.0
SWIGLU_ALPHA = 1.702
EXPERT_BLOCK = 256
V7X_VMEM_LIMIT = 56 * 1024 * 1024
HI = lax.Precision.HIGHEST
SCAN_EXACT = (True, True, False)

NN = ((1,), (0,))
NT = ((1,), (1,))
TN = ((0,), (0,))


def _dot(a, b, dims=NN, exact=False):
    dn = (dims, ((), ()))
    if exact:
        return lax.dot_general(a.astype(F32), b.astype(F32), dn, precision=HI, preferred_element_type=F32)
    return lax.dot_general(a.astype(BF16), b.astype(BF16), dn, preferred_element_type=F32)


def _sigmoid(x):
    return 1.0 / (1.0 + jnp.exp(-x))


def _params(n_grid_dims):
    return pltpu.CompilerParams(dimension_semantics=("arbitrary",) * n_grid_dims,
                                vmem_limit_bytes=V7X_VMEM_LIMIT)


def _const_spec(shape):
    nd = len(shape)
    return pl.BlockSpec(tuple(shape), lambda *_: (0,) * nd, pipeline_mode=pl.Buffered(1))


def _ada_kernel(c_ref, w_ref, b_ref, o_ref):
    cnd = c_ref[...]
    o_ref[...] = _dot(cnd * _sigmoid(cnd), w_ref[...], NN, True) + b_ref[...]


def ada_mods(cond, ada_w, ada_b):
    depth, dm, _ = ada_w.shape
    rows = cond.shape[0]
    out = pl.pallas_call(
        _ada_kernel,
        grid=(depth, N_MOD),
        in_specs=[pl.BlockSpec((rows, dm), lambda i, n: (0, 0)),
                  pl.BlockSpec((None, dm, dm), lambda i, n: (i, 0, n)),
                  pl.BlockSpec((None, 1, dm), lambda i, n: (i, 0, n))],
        out_specs=pl.BlockSpec((None, rows, dm), lambda i, n: (i, 0, n)),
        out_shape=jax.ShapeDtypeStruct((depth, rows, N_MOD * dm), F32),
        compiler_params=_params(2),
        name="ada_mods",
    )(cond, ada_w, ada_b.reshape(depth, 1, N_MOD * dm))
    return out.reshape(depth, rows, N_MOD, dm)


def _modulate(x, g, shift, scale):
    ms = jnp.mean(x * x, axis=-1, keepdims=True)
    return x * lax.rsqrt(ms + RMS_EPS) * g * (1.0 + scale) + shift


def _row_neighbours(h, row_len):
    tm = h.shape[0]
    pos = lax.broadcasted_iota(jnp.int32, (tm, 1), 0) & (row_len - 1)
    prev = jnp.where(pos != 0, pltpu.roll(h, 1, axis=0), 0.0)
    nxt = jnp.where(pos != row_len - 1, pltpu.roll(h, tm - 1, axis=0), 0.0)
    return prev, nxt


def _tile_row_len(n_ctx):
    return jnp.where(pl.program_id(1) == 0, n_ctx, GRID_W)


def _head_sum(z, hsel, hsel_t):
    zh = z.astype(BF16)
    zl = (z - zh.astype(F32)).astype(BF16)
    s = _dot(zh, hsel) + _dot(zl, hsel)
    sh = s.astype(BF16)
    sl = (s - sh.astype(F32)).astype(BF16)
    return _dot(sh, hsel_t) + _dot(sl, hsel_t)


def _moe_prologue(xn, mod_ref, ng2_ref, rw_ref, rb_ref, h2_ref, lg_ref):
    h2 = _modulate(xn, ng2_ref[...], mod_ref[3:4, :], mod_ref[4:5, :])
    h2_ref[...] = h2.astype(BF16)
    lg_ref[...] = _dot(h2, rw_ref[...], NN, True) + rb_ref[...]


def _tile_specs(dm):
    tile = pl.BlockSpec((None, TOKEN_TILE, dm), lambda b, t: (b, t, 0))
    ctx_row = lambda b, t: (jnp.where(t == 0, 4, b), 0, 0)
    mod = pl.BlockSpec((None, N_MOD, dm), ctx_row)
    return tile, mod


def _rwkv_front_kernel(x_ref, mod_ref, ng_ref, mu_ref, wrkv_ref, w1_ref, w2_ref, w0_ref, a1_ref,
                       a2_ref, a0_ref, g1_ref, g2_ref, kkw_ref, kaw_ref, rk_ref, hsel_ref, hselt_ref,
                       logw_ref, kd_ref, bv_ref, r_ref, v_ref, kk_ref, g_ref, bonus_ref, *, n_ctx):
    h = _modulate(x_ref[...], ng_ref[...], mod_ref[0:1, :], mod_ref[1:2, :])
    prev, nxt = _row_neighbours(h, _tile_row_len(n_ctx))
    d_prev = prev - h
    d_next = nxt - h

    def lerp(j):
        return h + mu_ref[2 * j:2 * j + 1, :] * d_prev + mu_ref[2 * j + 1:2 * j + 2, :] * d_next

    r = _dot(lerp(0), wrkv_ref[0])
    k = _dot(lerp(1), wrkv_ref[1])
    v = _dot(lerp(2), wrkv_ref[2])
    hw = jnp.tanh(_dot(lerp(3), w1_ref[...]))
    ha = _dot(lerp(4), a1_ref[...])
    g_ref[...] = _dot(_sigmoid(_dot(lerp(5), g1_ref[...])), g2_ref[...])

    hsel, hselt = hsel_ref[...], hselt_ref[...]
    kkraw = k * kkw_ref[...]
    kk = kkraw / jnp.maximum(jnp.sqrt(_head_sum(kkraw * kkraw, hsel, hselt)), 1e-12)
    ksum = jnp.zeros_like(k)
    for e in range(2):
        w_log = w0_ref[e:e + 1, :] + _dot(hw, w2_ref[e])
        logw_ref[e] = -math.exp(-0.5) * _sigmoid(w_log)
        a = _sigmoid(a0_ref[e:e + 1, :] + _dot(ha, a2_ref[e]))
        kd = k * (1.0 + (a - 1.0) * kaw_ref[...])
        kd_ref[e] = kd
        bv_ref[e] = kk * a
        ksum = ksum + kd
    r_ref[...] = r
    v_ref[...] = v
    kk_ref[...] = kk
    bonus_ref[...] = _head_sum(r * ksum * rk_ref[...], hsel, hselt) * v


def rwkv_front(xs, mods, norm_g, mu, w_rkv, w0, w1, w2, a0, a1, a2, g1, g2, k_k, k_a, r_k, hsel, hselt,
               n_ctx):
    bsz, length, dm = xs.shape
    lora_w, lora_a = w1.shape[-1], a1.shape[-1]
    tile, mod = _tile_specs(dm)
    dir_tile = pl.BlockSpec((2, None, TOKEN_TILE, dm), lambda b, t: (0, b, t, 0))

    def pad_rows(w, e):
        z = jnp.zeros_like(w[e])
        return jnp.concatenate([w[0] if e == 0 else z, w[1] if e == 1 else z], axis=0)

    w1c = jnp.concatenate([w1[0], w1[1]], axis=1).astype(BF16)
    a1c = jnp.concatenate([a1[0], a1[1]], axis=1).astype(BF16)
    w2p = jnp.stack([pad_rows(w2, 0), pad_rows(w2, 1)]).astype(BF16)
    a2p = jnp.stack([pad_rows(a2, 0), pad_rows(a2, 1)]).astype(BF16)
    vec = lambda a: a.reshape(1, dm)
    consts = [vec(norm_g), mu.reshape(12, dm), w_rkv.astype(BF16), w1c, w2p, w0, a1c, a2p, a0,
              g1.astype(BF16), g2.astype(BF16), vec(k_k), vec(k_a), vec(r_k), hsel, hselt]
    sds = jax.ShapeDtypeStruct
    return pl.pallas_call(
        functools.partial(_rwkv_front_kernel, n_ctx=n_ctx),
        grid=(bsz, length // TOKEN_TILE),
        in_specs=[tile, mod] + [_const_spec(a.shape) for a in consts],
        out_specs=[dir_tile, dir_tile, dir_tile, tile, tile, tile, tile, tile],
        out_shape=[sds((2, bsz, length, dm), F32)] * 3 + [sds((bsz, length, dm), F32)] * 5,
        compiler_params=_params(2),
        name="rwkv_front",
    )(xs, mods, *consts)


def _wkv_kernel(logw_ref, kd_ref, b_ref, r_ref, v_ref, kk_ref, y_ref, s_ref, *, exact):
    ex_a, ex_t, ex_s = exact
    d = pl.program_id(1)
    c = pl.program_id(2)
    n_groups = s_ref.shape[0]

    @pl.when(c == 0)
    def _():
        s_ref[...] = jnp.zeros_like(s_ref)

    sgn = 1 - 2 * d
    row = lax.broadcasted_iota(jnp.int32, (CHUNK, CHUNK), 0)
    col = lax.broadcasted_iota(jnp.int32, (CHUNK, CHUNK), 1)
    tri = ((row - col) * sgn >= 0).astype(F32)
    rowc = lax.broadcasted_iota(jnp.int32, (CHUNK, GL), 0)
    colc = lax.broadcasted_iota(jnp.int32, (CHUNK, GL), 1) % HEAD
    dcat = (rowc - colc) * sgn
    strict = dcat > 0
    incl = dcat >= 0
    eye = dcat == 0
    brow = lax.broadcasted_iota(jnp.int32, (GL, GL), 0) // HEAD
    bcol = lax.broadcasted_iota(jnp.int32, (GL, GL), 1) // HEAD
    bmask = brow == bcol

    def bd(y):
        return jnp.where(bmask, jnp.concatenate([y] * GROUP, axis=0), 0.0)

    def bdmm(x, y):
        return _dot(x, bd(y), NN, ex_t)

    lw = logw_ref[...]
    cum = _dot(tri, lw, NN, True)
    tot = jnp.sum(lw, axis=0, keepdims=True)
    e_pos = jnp.exp(cum)
    e_x = jnp.exp(cum - lw)
    e_neg = jnp.exp(-cum)
    e_tot = jnp.exp(tot)
    kkp = kk_ref[...] * e_x
    rp = r_ref[...] * e_pos
    kinv = kd_ref[...] * e_neg
    binv = b_ref[...] * e_neg
    kdec = kinv * e_tot
    bdec = binv * e_tot
    v = v_ref[...]

    for g in range(n_groups):
        sl = slice(g * GL, (g + 1) * GL)
        kkp_g, rp_g, v_g = kkp[:, sl], rp[:, sl], v[:, sl]
        lhs2 = jnp.concatenate([kkp_g, rp_g], axis=0)
        ak = _dot(lhs2, bd(kinv[:, sl]), NT, ex_a)
        ab = _dot(lhs2, bd(binv[:, sl]), NT, ex_a)
        akk = jnp.where(strict, ak[:CHUNK], 0.0)
        ark = jnp.where(incl, ak[CHUNK:], 0.0)
        abk = jnp.where(strict, ab[:CHUNK], 0.0)
        arb = jnp.where(incl, ab[CHUNK:], 0.0)
        npow = -abk
        t = jnp.where(eye, 1.0, 0.0) + npow
        m = 1
        while 2 * m < CHUNK:
            npow = bdmm(npow, npow)
            t = t + bdmm(t, npow)
            m *= 2
        w = bdmm(t, kkp_g)
        u0 = bdmm(t, bdmm(akk, v_g))
        qt = rp_g - bdmm(arb, w)
        y0 = bdmm(ark, v_g) - bdmm(arb, u0)
        s = s_ref[g]
        u = _dot(w, s, NT, ex_s) + u0
        y_ref[:, sl] = _dot(qt, s, NT, ex_s) + y0
        upd = _dot(jnp.concatenate([v_g, u], axis=0),
                   jnp.concatenate([kdec[:, sl], -bdec[:, sl]], axis=0), TN, ex_s)
        s_ref[g] = s * e_tot[:, sl] + jnp.where(bmask, upd, 0.0)


def wkv_scan(logw, kdir, bvec, r, v, kk, n_ctx, exact=SCAN_EXACT):
    _, bsz, length, dm = logw.shape
    nc = length // CHUNK
    ncc = n_ctx // CHUNK

    def pos(d, c):
        back = jnp.where(c < ncc, ncc - 1 - c, nc - 1 + ncc - c)
        return jnp.where(d == 0, c, back)

    dir_spec = pl.BlockSpec((None, None, CHUNK, dm), lambda b, d, c: (d, b, pos(d, c), 0))
    sh_spec = pl.BlockSpec((None, CHUNK, dm), lambda b, d, c: (b, pos(d, c), 0))
    return pl.pallas_call(
        functools.partial(_wkv_kernel, exact=exact),
        grid=(bsz, 2, nc),
        in_specs=[dir_spec, dir_spec, dir_spec, sh_spec, sh_spec, sh_spec],
        out_specs=dir_spec,
        out_shape=jax.ShapeDtypeStruct((2, bsz, length, dm), F32),
        scratch_shapes=[pltpu.VMEM((dm // GL, GL, GL), F32)],
        compiler_params=_params(3),
        name="wkv_scan",
    )(logw, kdir, bvec, r, v, kk)


def _rwkv_readout_kernel(x_ref, mod_ref, y_ref, bonus_ref, g_ref, lng_ref, lnb_ref, wo_ref, hsel_ref,
                         hselt_ref, ng2_ref, rw_ref, rb_ref, xo_ref, h2_ref, lg_ref):
    hsel, hselt = hsel_ref[...], hselt_ref[...]
    y = y_ref[0] + y_ref[1]
    yc = y - _head_sum(y, hsel, hselt) * (1.0 / HEAD)
    var = _head_sum(yc * yc, hsel, hselt) * (1.0 / HEAD)
    yn = yc * lax.rsqrt(var + GN_EPS) * lng_ref[...] + lnb_ref[...]
    z = (yn + bonus_ref[...]) * g_ref[...]
    xn = x_ref[...] + mod_ref[2:3, :] * _dot(z, wo_ref[...])
    xo_ref[...] = xn
    _moe_prologue(xn, mod_ref, ng2_ref, rw_ref, rb_ref, h2_ref, lg_ref)


def rwkv_readout(xs, mods, y, bonus, g, ln_g, ln_b, w_o, hsel, hselt, ng2, router_w, router_b):
    bsz, length, dm = xs.shape
    tile, mod = _tile_specs(dm)
    dir_tile = pl.BlockSpec((2, None, TOKEN_TILE, dm), lambda b, t: (0, b, t, 0))
    lg_tile = pl.BlockSpec((None, TOKEN_TILE, LANES), lambda b, t: (b, t, 0))
    vec = lambda a: a.reshape(1, -1)
    consts = [vec(ln_g), vec(ln_b), w_o.astype(BF16), hsel, hselt, vec(ng2), router_w, router_b]
    sds = jax.ShapeDtypeStruct
    return pl.pallas_call(
        _rwkv_readout_kernel,
        grid=(bsz, length // TOKEN_TILE),
        in_specs=[tile, mod, dir_tile, tile, tile] + [_const_spec(a.shape) for a in consts],
        out_specs=[tile, tile, lg_tile],
        out_shape=[sds((bsz, length, dm), F32), sds((bsz, length, dm), BF16),
                   sds((bsz, length, LANES), F32)],
        compiler_params=_params(2),
        name="rwkv_readout",
    )(xs, mods, y, bonus, g, *consts)


def _conv_kernel(x_ref, mod_ref, ng_ref, win_ref, cw_ref, wout_ref, ng2_ref, rw_ref, rb_ref,
                 xo_ref, h2_ref, lg_ref, *, n_ctx):
    x = x_ref[...]
    dm = x.shape[1]
    h = _modulate(x, ng_ref[...], mod_ref[0:1, :], mod_ref[1:2, :])
    hw = _dot(h, win_ref[...])
    b_gate, c_gate, u = hw[:, :dm], hw[:, dm:2 * dm], hw[:, 2 * dm:]
    z = c_gate * u
    z_prev, z_next = _row_neighbours(z, _tile_row_len(n_ctx))
    z = cw_ref[0:1, :] * z_prev + cw_ref[1:2, :] * z + cw_ref[2:3, :] * z_next
    xn = x + mod_ref[2:3, :] * _dot(b_gate * z, wout_ref[...])
    xo_ref[...] = xn
    _moe_prologue(xn, mod_ref, ng2_ref, rw_ref, rb_ref, h2_ref, lg_ref)


def conv_mixer(xs, mods, ng1, w_in, conv_w, w_out, ng2, router_w, router_b, n_ctx):
    bsz, length, dm = xs.shape
    tile, mod = _tile_specs(dm)
    lg_tile = pl.BlockSpec((None, TOKEN_TILE, LANES), lambda b, t: (b, t, 0))
    vec = lambda a: a.reshape(1, -1)
    consts = [vec(ng1), w_in.astype(BF16), conv_w, w_out.astype(BF16), vec(ng2), router_w, router_b]
    sds = jax.ShapeDtypeStruct
    return pl.pallas_call(
        functools.partial(_conv_kernel, n_ctx=n_ctx),
        grid=(bsz, length // TOKEN_TILE),
        in_specs=[tile, mod] + [_const_spec(a.shape) for a in consts],
        out_specs=[tile, tile, lg_tile],
        out_shape=[sds((bsz, length, dm), F32), sds((bsz, length, dm), BF16),
                   sds((bsz, length, LANES), F32)],
        compiler_params=_params(2),
        name="conv_mixer",
    )(xs, mods, *consts)


def _expert_kernel(be_ref, xb_ref, wgu_ref, bgu_ref, wd_ref, bdn_ref, o_ref):
    del be_ref
    hgu = _dot(xb_ref[...], wgu_ref[...]) + bgu_ref[...]
    ff = hgu.shape[1] // 2
    g_ = jnp.minimum(hgu[:, :ff], SWIGLU_LIMIT)
    u_ = jnp.clip(hgu[:, ff:], -SWIGLU_LIMIT, SWIGLU_LIMIT)
    act = (u_ + 1.0) * (g_ * _sigmoid(SWIGLU_ALPHA * g_))
    o_ref[...] = _dot(act, wd_ref[...]) + bdn_ref[...]


def expert_ffn(block_e, xb, w_gu, b_gu, w_down, b_down):
    n_rows, dm = xb.shape
    n_e, _, ff2 = w_gu.shape
    grid_spec = pltpu.PrefetchScalarGridSpec(
        num_scalar_prefetch=1,
        grid=(n_rows // EXPERT_BLOCK,),
        in_specs=[pl.BlockSpec((EXPERT_BLOCK, dm), lambda i, be: (i, 0)),
                  pl.BlockSpec((None, dm, ff2), lambda i, be: (be[i], 0, 0)),
                  pl.BlockSpec((None, 1, ff2), lambda i, be: (be[i], 0, 0)),
                  pl.BlockSpec((None, ff2 // 2, dm), lambda i, be: (be[i], 0, 0)),
                  pl.BlockSpec((None, 1, dm), lambda i, be: (be[i], 0, 0))],
        out_specs=pl.BlockSpec((EXPERT_BLOCK, dm), lambda i, be: (i, 0)))
    return pl.pallas_call(
        _expert_kernel,
        grid_spec=grid_spec,
        out_shape=jax.ShapeDtypeStruct((n_rows, dm), F32),
        compiler_params=_params(1),
        name="expert_ffn",
    )(block_e, xb, w_gu, b_gu.reshape(n_e, 1, ff2), w_down, b_down.reshape(n_e, 1, dm))


def moe_ffn(h2, logits, w_gu, b_gu, w_down, b_down):
    t, dm = h2.shape
    top_logit, top_e = lax.top_k(logits[:, :N_EXPERTS], TOP_K)
    gate = jax.nn.softmax(top_logit, axis=-1)
    expert = top_e.reshape(-1).astype(jnp.int32)
    n_assign = t * TOP_K
    counts = jnp.zeros((N_EXPERTS,), jnp.int32).at[expert].add(1)
    padded = (counts + EXPERT_BLOCK - 1) // EXPERT_BLOCK * EXPERT_BLOCK
    start = jnp.cumsum(counts) - counts
    pad_end = jnp.cumsum(padded)
    pad_start = pad_end - padded
    order = jnp.argsort(expert)
    e_sorted = expert[order]
    dest_sorted = pad_start[e_sorted] + jnp.arange(n_assign, dtype=jnp.int32) - start[e_sorted]
    n_blocks = -(-(n_assign + N_EXPERTS * (EXPERT_BLOCK - 1)) // EXPERT_BLOCK)
    n_rows = n_blocks * EXPERT_BLOCK
    row_tok = jnp.zeros((n_rows,), jnp.int32).at[dest_sorted].set(order // TOP_K)
    dest = jnp.zeros((n_assign,), jnp.int32).at[order].set(dest_sorted)
    block_start = jnp.arange(n_blocks, dtype=jnp.int32) * EXPERT_BLOCK
    block_e = jnp.minimum(jnp.searchsorted(pad_end, block_start, side='right'),
                          N_EXPERTS - 1).astype(jnp.int32)
    yb = expert_ffn(block_e, h2[row_tok], w_gu, b_gu, w_down, b_down)
    return jnp.sum(yb[dest].reshape(t, TOP_K, dm) * gate[..., None], axis=1)


def _final_norm_kernel(x_ref, g_ref, o_ref):
    x = x_ref[...]
    ms = jnp.mean(x * x, axis=-1, keepdims=True)
    o_ref[...] = x * lax.rsqrt(ms + RMS_EPS) * g_ref[...]


def final_norm(xs, g, n_ctx):
    bsz, length, dm = xs.shape
    skip = n_ctx // TOKEN_TILE
    return pl.pallas_call(
        _final_norm_kernel,
        grid=(bsz, length // TOKEN_TILE - skip),
        in_specs=[pl.BlockSpec((None, TOKEN_TILE, dm), lambda b, t: (b, t + skip, 0)),
                  pl.BlockSpec((1, dm), lambda b, t: (0, 0))],
        out_specs=pl.BlockSpec((None, TOKEN_TILE, dm), lambda b, t: (b, t, 0)),
        out_shape=jax.ShapeDtypeStruct((bsz, length - n_ctx, dm), F32),
        compiler_params=_params(2),
        name="final_norm",
    )(xs, g.reshape(1, dm))


def kernel(x, c, ctx, c_ctx, ada_w, ada_b, norm_g, final_norm_g, rw_mu, rw_w_rkv, rw_w0, rw_w1, rw_w2, rw_a0, rw_a1, rw_a2, rw_g1, rw_g2, rw_k_k, rw_k_a, rw_r_k, rw_ln_g, rw_ln_b, rw_w_o, sc_w_in, sc_conv, sc_w_out, moe_router_w, moe_router_b, moe_w_gu, moe_b_gu, moe_w_down, moe_b_down):
    bsz, seq, dm = x.shape
    n_ctx = ctx.shape[1]
    depth = ada_w.shape[0]
    assert n_ctx == TOKEN_TILE and seq % TOKEN_TILE == 0 and dm % GL == 0 and bsz == 4
    length = n_ctx + seq

    cond = jnp.concatenate([c, c_ctx[None], jnp.zeros((3, dm), F32)], axis=0)
    mods = ada_mods(cond, ada_w, ada_b)
    xs = jnp.concatenate([ctx, x], axis=1)

    head_of_lane = jnp.arange(dm, dtype=jnp.int32) // HEAD
    hsel = (head_of_lane[:, None] == jnp.arange(LANES, dtype=jnp.int32)[None, :]).astype(BF16)
    hselt = hsel.T
    gate2 = lambda i: jnp.concatenate(
        [jnp.broadcast_to(mods[i, 4, 5][None, None], (bsz, n_ctx, dm)),
         jnp.broadcast_to(mods[i, :bsz, 5][:, None], (bsz, seq, dm))], axis=1)

    for i in range(depth):
        j = i // 2
        router_w = jnp.pad(moe_router_w[i], ((0, 0), (0, LANES - N_EXPERTS)))
        router_b = jnp.pad(moe_router_b[i], (0, LANES - N_EXPERTS)).reshape(1, LANES)
        if i % 2 == 0:
            logw, kdir, bvec, r, v, kk, g, bonus = rwkv_front(
                xs, mods[i], norm_g[i, 0], rw_mu[j], rw_w_rkv[j], rw_w0[j], rw_w1[j], rw_w2[j], rw_a0[j],
                rw_a1[j], rw_a2[j], rw_g1[j], rw_g2[j], rw_k_k[j], rw_k_a[j], rw_r_k[j], hsel, hselt, n_ctx)
            y = wkv_scan(logw, kdir, bvec, r, v, kk, n_ctx)
            xs, h2, logits = rwkv_readout(xs, mods[i], y, bonus, g, rw_ln_g[j], rw_ln_b[j], rw_w_o[j],
                                          hsel, hselt, norm_g[i, 1], router_w, router_b)
        else:
            xs, h2, logits = conv_mixer(xs, mods[i], norm_g[i, 0], sc_w_in[j], sc_conv[j], sc_w_out[j],
                                        norm_g[i, 1], router_w, router_b, n_ctx)
        f = moe_ffn(h2.reshape(bsz * length, dm), logits.reshape(bsz * length, LANES),
                    moe_w_gu[i].astype(BF16), moe_b_gu[i], moe_w_down[i].astype(BF16), moe_b_down[i])
        xs = xs + gate2(i) * f.reshape(bsz, length, dm)
    return final_norm(xs, final_norm_g, n_ctx)
```

```python
import functools
import math

import jax
import jax.numpy as jnp
from jax import lax
from jax.experimental import pallas as pl
from jax.experimental.pallas import tpu as pltpu

F32 = jnp.float32
BF16 = jnp.bfloat16

HEAD = 64
CHUNK = 64
GROUP = 2
GL = GROUP * HEAD
LANES = 128
GRID_W = 64
TOKEN_TILE = 256
RMS_EPS = 1e-6
GN_EPS = HEAD * 1e-5
N_EXPERTS = 32
TOP_K = 4
N_MOD = 6
SWIGLU_LIMIT = 7.0
SWIGLU_ALPHA = 1.702
EXPERT_BLOCK = 256
V7X_VMEM_LIMIT = 56 * 1024 * 1024
HI = lax.Precision.HIGHEST
SCAN_EXACT = (False, False, False)

NN = ((1,), (0,))
NT = ((1,), (1,))
TN = ((0,), (0,))


def _dot(a, b, dims=NN, exact=False):
    dn = (dims, ((), ()))
    if exact:
        return lax.dot_general(a.astype(F32), b.astype(F32), dn, precision=HI, preferred_element_type=F32)
    return lax.dot_general(a.astype(BF16), b.astype(BF16), dn, preferred_element_type=F32)


def _sigmoid(x):
    return 1.0 / (1.0 + jnp.exp(-x))


def _params(n_grid_dims):
    return pltpu.CompilerParams(dimension_semantics=("arbitrary",) * n_grid_dims,
                                vmem_limit_bytes=V7X_VMEM_LIMIT)


def _const_spec(shape):
    nd = len(shape)
    return pl.BlockSpec(tuple(shape), lambda *_: (0,) * nd, pipeline_mode=pl.Buffered(1))


def _ada_kernel(c_ref, w_ref, b_ref, o_ref):
    cnd = c_ref[...]
    o_ref[...] = _dot(cnd * _sigmoid(cnd), w_ref[...], NN, True) + b_ref[...]


def ada_mods(cond, ada_w, ada_b):
    depth, dm, _ = ada_w.shape
    rows = cond.shape[0]
    out = pl.pallas_call(
        _ada_kernel,
        grid=(depth, N_MOD),
        in_specs=[pl.BlockSpec((rows, dm), lambda i, n: (0, 0)),
                  pl.BlockSpec((None, dm, dm), lambda i, n: (i, 0, n)),
                  pl.BlockSpec((None, 1, dm), lambda i, n: (i, 0, n))],
        out_specs=pl.BlockSpec((None, rows, dm), lambda i, n: (i, 0, n)),
        out_shape=jax.ShapeDtypeStruct((depth, rows, N_MOD * dm), F32),
        compiler_params=_params(2),
        name="ada_mods",
    )(cond, ada_w, ada_b.reshape(depth, 1, N_MOD * dm))
    return out.reshape(depth, rows, N_MOD, dm)


def _modulate(x, g, shift, scale):
    ms = jnp.mean(x * x, axis=-1, keepdims=True)
    return x * lax.rsqrt(ms + RMS_EPS) * g * (1.0 + scale) + shift


def _row_neighbours(h, row_len):
    tm = h.shape[0]
    pos = lax.broadcasted_iota(jnp.int32, (tm, 1), 0) & (row_len - 1)
    prev = jnp.where(pos != 0, pltpu.roll(h, 1, axis=0), 0.0)
    nxt = jnp.where(pos != row_len - 1, pltpu.roll(h, tm - 1, axis=0), 0.0)
    return prev, nxt


def _tile_row_len(n_ctx):
    return jnp.where(pl.program_id(1) == 0, n_ctx, GRID_W)


def _head_sum(z, hsel, hsel_t):
    zh = z.astype(BF16)
    zl = (z - zh.astype(F32)).astype(BF16)
    s = _dot(zh, hsel) + _dot(zl, hsel)
    sh = s.astype(BF16)
    sl = (s - sh.astype(F32)).astype(BF16)
    return _dot(sh, hsel_t) + _dot(sl, hsel_t)


def _moe_prologue(xn, mod_ref, ng2_ref, rw_ref, rb_ref, h2_ref, lg_ref):
    h2 = _modulate(xn, ng2_ref[...], mod_ref[3:4, :], mod_ref[4:5, :])
    h2_ref[...] = h2.astype(BF16)
    lg_ref[...] = _dot(h2, rw_ref[...], NN, True) + rb_ref[...]


def _tile_specs(dm):
    tile = pl.BlockSpec((None, TOKEN_TILE, dm), lambda b, t: (b, t, 0))
    ctx_row = lambda b, t: (jnp.where(t == 0, 4, b), 0, 0)
    mod = pl.BlockSpec((None, N_MOD, dm), ctx_row)
    return tile, mod


def _rwkv_front_kernel(x_ref, mod_ref, ng_ref, mu_ref, wrkv_ref, w1_ref, w2_ref, w0_ref, a1_ref,
                       a2_ref, a0_ref, g1_ref, g2_ref, kkw_ref, kaw_ref, rk_ref, hsel_ref, hselt_ref,
                       logw_ref, kd_ref, bv_ref, r_ref, v_ref, kk_ref, g_ref, bonus_ref, *, n_ctx):
    h = _modulate(x_ref[...], ng_ref[...], mod_ref[0:1, :], mod_ref[1:2, :])
    prev, nxt = _row_neighbours(h, _tile_row_len(n_ctx))
    d_prev = prev - h
    d_next = nxt - h

    def lerp(j):
        return h + mu_ref[2 * j:2 * j + 1, :] * d_prev + mu_ref[2 * j + 1:2 * j + 2, :] * d_next

    r = _dot(lerp(0), wrkv_ref[0])
    k = _dot(lerp(1), wrkv_ref[1])
    v = _dot(lerp(2), wrkv_ref[2])
    hw = jnp.tanh(_dot(lerp(3), w1_ref[...]))
    ha = _dot(lerp(4), a1_ref[...])
    g_ref[...] = _dot(_sigmoid(_dot(lerp(5), g1_ref[...])), g2_ref[...])

    hsel, hselt = hsel_ref[...], hselt_ref[...]
    kkraw = k * kkw_ref[...]
    kk = kkraw / jnp.maximum(jnp.sqrt(_head_sum(kkraw * kkraw, hsel, hselt)), 1e-12)
    ksum = jnp.zeros_like(k)
    for e in range(2):
        w_log = w0_ref[e:e + 1, :] + _dot(hw, w2_ref[e])
        logw_ref[e] = -math.exp(-0.5) * _sigmoid(w_log)
        a = _sigmoid(a0_ref[e:e + 1, :] + _dot(ha, a2_ref[e]))
        kd = k * (1.0 + (a - 1.0) * kaw_ref[...])
        kd_ref[e] = kd
        bv_ref[e] = kk * a
        ksum = ksum + kd
    r_ref[...] = r
    v_ref[...] = v
    kk_ref[...] = kk
    bonus_ref[...] = _head_sum(r * ksum * rk_ref[...], hsel, hselt) * v


def rwkv_front(xs, mods, norm_g, mu, w_rkv, w0, w1, w2, a0, a1, a2, g1, g2, k_k, k_a, r_k, hsel, hselt,
               n_ctx):
    bsz, length, dm = xs.shape
    lora_w, lora_a = w1.shape[-1], a1.shape[-1]
    tile, mod = _tile_specs(dm)
    dir_tile = pl.BlockSpec((2, None, TOKEN_TILE, dm), lambda b, t: (0, b, t, 0))

    def pad_rows(w, e):
        z = jnp.zeros_like(w[e])
        return jnp.concatenate([w[0] if e == 0 else z, w[1] if e == 1 else z], axis=0)

    w1c = jnp.concatenate([w1[0], w1[1]], axis=1).astype(BF16)
    a1c = jnp.concatenate([a1[0], a1[1]], axis=1).astype(BF16)
    w2p = jnp.stack([pad_rows(w2, 0), pad_rows(w2, 1)]).astype(BF16)
    a2p = jnp.stack([pad_rows(a2, 0), pad_rows(a2, 1)]).astype(BF16)
    vec = lambda a: a.reshape(1, dm)
    consts = [vec(norm_g), mu.reshape(12, dm), w_rkv.astype(BF16), w1c, w2p, w0, a1c, a2p, a0,
              g1.astype(BF16), g2.astype(BF16), vec(k_k), vec(k_a), vec(r_k), hsel, hselt]
    sds = jax.ShapeDtypeStruct
    return pl.pallas_call(
        functools.partial(_rwkv_front_kernel, n_ctx=n_ctx),
        grid=(bsz, length // TOKEN_TILE),
        in_specs=[tile, mod] + [_const_spec(a.shape) for a in consts],
        out_specs=[dir_tile, dir_tile, dir_tile, tile, tile, tile, tile, tile],
        out_shape=[sds((2, bsz, length, dm), F32)] * 3 + [sds((bsz, length, dm), F32)] * 5,
        compiler_params=_params(2),
        name="rwkv_front",
    )(xs, mods, *consts)


def _wkv_kernel(logw_ref, kd_ref, b_ref, r_ref, v_ref, kk_ref, y_ref, s_ref, *, exact):
    ex_a, ex_t, ex_s = exact
    d = pl.program_id(1)
    c = pl.program_id(2)
    n_groups = s_ref.shape[0]

    @pl.when(c == 0)
    def _():
        s_ref[...] = jnp.zeros_like(s_ref)

    sgn = 1 - 2 * d
    row = lax.broadcasted_iota(jnp.int32, (CHUNK, CHUNK), 0)
    col = lax.broadcasted_iota(jnp.int32, (CHUNK, CHUNK), 1)
    tri = ((row - col) * sgn >= 0).astype(F32)
    rowc = lax.broadcasted_iota(jnp.int32, (CHUNK, GL), 0)
    colc = lax.broadcasted_iota(jnp.int32, (CHUNK, GL), 1) % HEAD
    dcat = (rowc - colc) * sgn
    strict = dcat > 0
    incl = dcat >= 0
    eye = dcat == 0
    brow = lax.broadcasted_iota(jnp.int32, (GL, GL), 0) // HEAD
    bcol = lax.broadcasted_iota(jnp.int32, (GL, GL), 1) // HEAD
    bmask = brow == bcol

    def bd(y):
        return jnp.where(bmask, jnp.concatenate([y] * GROUP, axis=0), 0.0)

    def bdmm(x, y):
        return _dot(x, bd(y), NN, ex_t)

    lw = logw_ref[...]
    cum = _dot(tri, lw, NN, True)
    tot = jnp.sum(lw, axis=0, keepdims=True)
    e_pos = jnp.exp(cum)
    e_x = jnp.exp(cum - lw)
    e_neg = jnp.exp(-cum)
    e_tot = jnp.exp(tot)
    kkp = kk_ref[...] * e_x
    rp = r_ref[...] * e_pos
    kinv = kd_ref[...] * e_neg
    binv = b_ref[...] * e_neg
    kdec = kinv * e_tot
    bdec = binv * e_tot
    v = v_ref[...]

    gs = range(n_groups)
    sls = [slice(g * GL, (g + 1) * GL) for g in gs]
    v_g = [v[:, sl] for sl in sls]
    lhs2 = [jnp.concatenate([kkp[:, sl], rp[:, sl]], axis=0) for sl in sls]
    ak = [_dot(lhs2[g], bd(kinv[:, sls[g]]), NT, ex_a) for g in gs]
    ab = [_dot(lhs2[g], bd(binv[:, sls[g]]), NT, ex_a) for g in gs]
    a_kr = [jnp.concatenate([jnp.where(strict, ak[g][:CHUNK], 0.0),
                             jnp.where(incl, ak[g][CHUNK:], 0.0)], axis=0) for g in gs]
    arb = [jnp.where(incl, ab[g][CHUNK:], 0.0) for g in gs]
    mp = [-jnp.where(strict, ab[g][:CHUNK], 0.0) for g in gs]
    t = [jnp.where(eye, 1.0, 0.0) + mp[g] for g in gs]
    mp = [bdmm(mp[g], mp[g]) for g in gs]
    m = 2
    while 2 * m < CHUNK:
        r2 = [bdmm(jnp.concatenate([mp[g], t[g]], axis=0), mp[g]) for g in gs]
        mp = [r2[g][:CHUNK] for g in gs]
        t = [t[g] + r2[g][CHUNK:] for g in gs]
        m *= 2
    t = [t[g] + bdmm(t[g], mp[g]) for g in gs]
    av = [bdmm(a_kr[g], v_g[g]) for g in gs]
    s = [s_ref[g] for g in gs]
    zs = [_dot(lhs2[g], s[g], NT, ex_s) for g in gs]
    u = [bdmm(t[g], zs[g][:CHUNK] + av[g][:CHUNK]) for g in gs]
    for g in gs:
        y_ref[:, sls[g]] = zs[g][CHUNK:] + av[g][CHUNK:] - bdmm(arb[g], u[g])
    upd = [_dot(jnp.concatenate([v_g[g], u[g]], axis=0),
                jnp.concatenate([kdec[:, sls[g]], -bdec[:, sls[g]]], axis=0), TN, ex_s) for g in gs]
    for g in gs:
        s_ref[g] = s[g] * e_tot[:, sls[g]] + jnp.where(bmask, upd[g], 0.0)


def wkv_scan(logw, kdir, bvec, r, v, kk, n_ctx, exact=SCAN_EXACT):
    _, bsz, length, dm = logw.shape
    nc = length // CHUNK
    ncc = n_ctx // CHUNK

    def pos(d, c):
        back = jnp.where(c < ncc, ncc - 1 - c, nc - 1 + ncc - c)
        return jnp.where(d == 0, c, back)

    dir_spec = pl.BlockSpec((None, None, CHUNK, dm), lambda b, d, c: (d, b, pos(d, c), 0))
    sh_spec = pl.BlockSpec((None, CHUNK, dm), lambda b, d, c: (b, pos(d, c), 0))
    return pl.pallas_call(
        functools.partial(_wkv_kernel, exact=exact),
        grid=(bsz, 2, nc),
        in_specs=[dir_spec, dir_spec, dir_spec, sh_spec, sh_spec, sh_spec],
        out_specs=dir_spec,
        out_shape=jax.ShapeDtypeStruct((2, bsz, length, dm), F32),
        scratch_shapes=[pltpu.VMEM((dm // GL, GL, GL), F32)],
        compiler_params=_params(3),
        name="wkv_scan",
    )(logw, kdir, bvec, r, v, kk)


def _rwkv_readout_kernel(x_ref, mod_ref, y_ref, bonus_ref, g_ref, lng_ref, lnb_ref, wo_ref, hsel_ref,
                         hselt_ref, ng2_ref, rw_ref, rb_ref, xo_ref, h2_ref, lg_ref):
    hsel, hselt = hsel_ref[...], hselt_ref[...]
    y = y_ref[0] + y_ref[1]
    yc = y - _head_sum(y, hsel, hselt) * (1.0 / HEAD)
    var = _head_sum(yc * yc, hsel, hselt) * (1.0 / HEAD)
    yn = yc * lax.rsqrt(var + GN_EPS) * lng_ref[...] + lnb_ref[...]
    z = (yn + bonus_ref[...]) * g_ref[...]
    xn = x_ref[...] + mod_ref[2:3, :] * _dot(z, wo_ref[...])
    xo_ref[...] = xn
    _moe_prologue(xn, mod_ref, ng2_ref, rw_ref, rb_ref, h2_ref, lg_ref)


def rwkv_readout(xs, mods, y, bonus, g, ln_g, ln_b, w_o, hsel, hselt, ng2, router_w, router_b):
    bsz, length, dm = xs.shape
    tile, mod = _tile_specs(dm)
    dir_tile = pl.BlockSpec((2, None, TOKEN_TILE, dm), lambda b, t: (0, b, t, 0))
    lg_tile = pl.BlockSpec((None, TOKEN_TILE, LANES), lambda b, t: (b, t, 0))
    vec = lambda a: a.reshape(1, -1)
    consts = [vec(ln_g), vec(ln_b), w_o.astype(BF16), hsel, hselt, vec(ng2), router_w, router_b]
    sds = jax.ShapeDtypeStruct
    return pl.pallas_call(
        _rwkv_readout_kernel,
        grid=(bsz, length // TOKEN_TILE),
        in_specs=[tile, mod, dir_tile, tile, tile] + [_const_spec(a.shape) for a in consts],
        out_specs=[tile, tile, lg_tile],
        out_shape=[sds((bsz, length, dm), F32), sds((bsz, length, dm), BF16),
                   sds((bsz, length, LANES), F32)],
        compiler_params=_params(2),
        name="rwkv_readout",
    )(xs, mods, y, bonus, g, *consts)


def _conv_kernel(x_ref, mod_ref, ng_ref, win_ref, cw_ref, wout_ref, ng2_ref, rw_ref, rb_ref,
                 xo_ref, h2_ref, lg_ref, *, n_ctx):
    x = x_ref[...]
    dm = x.shape[1]
    h = _modulate(x, ng_ref[...], mod_ref[0:1, :], mod_ref[1:2, :])
    hw = _dot(h, win_ref[...])
    b_gate, c_gate, u = hw[:, :dm], hw[:, dm:2 * dm], hw[:, 2 * dm:]
    z = c_gate * u
    z_prev, z_next = _row_neighbours(z, _tile_row_len(n_ctx))
    z = cw_ref[0:1, :] * z_prev + cw_ref[1:2, :] * z + cw_ref[2:3, :] * z_next
    xn = x + mod_ref[2:3, :] * _dot(b_gate * z, wout_ref[...])
    xo_ref[...] = xn
    _moe_prologue(xn, mod_ref, ng2_ref, rw_ref, rb_ref, h2_ref, lg_ref)


def conv_mixer(xs, mods, ng1, w_in, conv_w, w_out, ng2, router_w, router_b, n_ctx):
    bsz, length, dm = xs.shape
    tile, mod = _tile_specs(dm)
    lg_tile = pl.BlockSpec((None, TOKEN_TILE, LANES), lambda b, t: (b, t, 0))
    vec = lambda a: a.reshape(1, -1)
    consts = [vec(ng1), w_in.astype(BF16), conv_w, w_out.astype(BF16), vec(ng2), router_w, router_b]
    sds = jax.ShapeDtypeStruct
    return pl.pallas_call(
        functools.partial(_conv_kernel, n_ctx=n_ctx),
        grid=(bsz, length // TOKEN_TILE),
        in_specs=[tile, mod] + [_const_spec(a.shape) for a in consts],
        out_specs=[tile, tile, lg_tile],
        out_shape=[sds((bsz, length, dm), F32), sds((bsz, length, dm), BF16),
                   sds((bsz, length, LANES), F32)],
        compiler_params=_params(2),
        name="conv_mixer",
    )(xs, mods, *consts)


def _expert_kernel(be_ref, xb_ref, wgu_ref, bgu_ref, wd_ref, bdn_ref, o_ref, wgu_bf, wd_bf):
    i = pl.program_id(0)
    new_expert = jnp.logical_or(i == 0, be_ref[i] != be_ref[jnp.maximum(i - 1, 0)])

    @pl.when(new_expert)
    def _():
        wgu_bf[...] = wgu_ref[...].astype(BF16)
        wd_bf[...] = wd_ref[...].astype(BF16)

    hgu = _dot(xb_ref[...], wgu_bf[...]) + bgu_ref[...]
    ff = hgu.shape[1] // 2
    g_ = jnp.minimum(hgu[:, :ff], SWIGLU_LIMIT)
    u_ = jnp.clip(hgu[:, ff:], -SWIGLU_LIMIT, SWIGLU_LIMIT)
    act = (u_ + 1.0) * (g_ * _sigmoid(SWIGLU_ALPHA * g_))
    o_ref[...] = _dot(act, wd_bf[...]) + bdn_ref[...]


def expert_ffn(block_e, xb, w_gu, b_gu, w_down, b_down):
    n_rows, dm = xb.shape
    n_e, _, ff2 = w_gu.shape
    grid_spec = pltpu.PrefetchScalarGridSpec(
        num_scalar_prefetch=1,
        grid=(n_rows // EXPERT_BLOCK,),
        in_specs=[pl.BlockSpec((EXPERT_BLOCK, dm), lambda i, be: (i, 0)),
                  pl.BlockSpec((None, dm, ff2), lambda i, be: (be[i], 0, 0)),
                  pl.BlockSpec((None, 1, ff2), lambda i, be: (be[i], 0, 0)),
                  pl.BlockSpec((None, ff2 // 2, dm), lambda i, be: (be[i], 0, 0)),
                  pl.BlockSpec((None, 1, dm), lambda i, be: (be[i], 0, 0))],
        out_specs=pl.BlockSpec((EXPERT_BLOCK, dm), lambda i, be: (i, 0)),
        scratch_shapes=[pltpu.VMEM((dm, ff2), BF16), pltpu.VMEM((ff2 // 2, dm), BF16)])
    return pl.pallas_call(
        _expert_kernel,
        grid_spec=grid_spec,
        out_shape=jax.ShapeDtypeStruct((n_rows, dm), F32),
        compiler_params=_params(1),
        name="expert_ffn",
    )(block_e, xb, w_gu, b_gu.reshape(n_e, 1, ff2), w_down, b_down.reshape(n_e, 1, dm))


def moe_ffn(h2, logits, w_gu, b_gu, w_down, b_down):
    t, dm = h2.shape
    top_logit, top_e = lax.top_k(logits[:, :N_EXPERTS], TOP_K)
    gate = jax.nn.softmax(top_logit, axis=-1)
    expert = top_e.reshape(-1).astype(jnp.int32)
    n_assign = t * TOP_K
    eids = jnp.arange(N_EXPERTS, dtype=jnp.int32)
    counts = jnp.sum((expert[:, None] == eids[None, :]).astype(jnp.int32), axis=0)
    padded = (counts + EXPERT_BLOCK - 1) // EXPERT_BLOCK * EXPERT_BLOCK
    start = jnp.cumsum(counts) - counts
    pad_end = jnp.cumsum(padded)
    pad_start = pad_end - padded
    order = jnp.argsort(expert)
    rank = jnp.argsort(order).astype(jnp.int32)
    dest = pad_start[expert] + rank - start[expert]
    n_blocks = -(-(n_assign + N_EXPERTS * (EXPERT_BLOCK - 1)) // EXPERT_BLOCK)
    n_rows = n_blocks * EXPERT_BLOCK
    block_start = jnp.arange(n_blocks, dtype=jnp.int32) * EXPERT_BLOCK
    block_e = jnp.minimum(jnp.sum((pad_end[None, :] <= block_start[:, None]).astype(jnp.int32), axis=1),
                          N_EXPERTS - 1)
    row_e = jnp.repeat(block_e, EXPERT_BLOCK)
    row_q = jnp.arange(n_rows, dtype=jnp.int32) - pad_start[row_e]
    row_valid = row_q < counts[row_e]
    row_src = jnp.where(row_valid, start[row_e] + row_q, 0)
    row_tok = jnp.where(row_valid, order[row_src] // TOP_K, 0)
    yb = expert_ffn(block_e, h2[row_tok], w_gu, b_gu, w_down, b_down)
    dest_km = dest.reshape(t, TOP_K).T
    f = yb[dest_km[0]] * gate[:, 0:1]
    for k in range(1, TOP_K):
        f = f + yb[dest_km[k]] * gate[:, k:k + 1]
    return f


def _final_norm_kernel(x_ref, g_ref, o_ref):
    x = x_ref[...]
    ms = jnp.mean(x * x, axis=-1, keepdims=True)
    o_ref[...] = x * lax.rsqrt(ms + RMS_EPS) * g_ref[...]


def final_norm(xs, g, n_ctx):
    bsz, length, dm = xs.shape
    skip = n_ctx // TOKEN_TILE
    return pl.pallas_call(
        _final_norm_kernel,
        grid=(bsz, length // TOKEN_TILE - skip),
        in_specs=[pl.BlockSpec((None, TOKEN_TILE, dm), lambda b, t: (b, t + skip, 0)),
                  pl.BlockSpec((1, dm), lambda b, t: (0, 0))],
        out_specs=pl.BlockSpec((None, TOKEN_TILE, dm), lambda b, t: (b, t, 0)),
        out_shape=jax.ShapeDtypeStruct((bsz, length - n_ctx, dm), F32),
        compiler_params=_params(2),
        name="final_norm",
    )(xs, g.reshape(1, dm))


def kernel(x, c, ctx, c_ctx, ada_w, ada_b, norm_g, final_norm_g, rw_mu, rw_w_rkv, rw_w0, rw_w1, rw_w2, rw_a0, rw_a1, rw_a2, rw_g1, rw_g2, rw_k_k, rw_k_a, rw_r_k, rw_ln_g, rw_ln_b, rw_w_o, sc_w_in, sc_conv, sc_w_out, moe_router_w, moe_router_b, moe_w_gu, moe_b_gu, moe_w_down, moe_b_down):
    bsz, seq, dm = x.shape
    n_ctx = ctx.shape[1]
    depth = ada_w.shape[0]
    assert n_ctx == TOKEN_TILE and seq % TOKEN_TILE == 0 and dm % GL == 0 and bsz == 4
    length = n_ctx + seq

    cond = jnp.concatenate([c, c_ctx[None], jnp.zeros((3, dm), F32)], axis=0)
    mods = ada_mods(cond, ada_w, ada_b)
    xs = jnp.concatenate([ctx, x], axis=1)

    head_of_lane = jnp.arange(dm, dtype=jnp.int32) // HEAD
    hsel = (head_of_lane[:, None] == jnp.arange(LANES, dtype=jnp.int32)[None, :]).astype(BF16)
    hselt = hsel.T
    gate2 = lambda i: jnp.concatenate(
        [jnp.broadcast_to(mods[i, 4, 5][None, None], (bsz, n_ctx, dm)),
         jnp.broadcast_to(mods[i, :bsz, 5][:, None], (bsz, seq, dm))], axis=1)

    for i in range(depth):
        j = i // 2
        router_w = jnp.pad(moe_router_w[i], ((0, 0), (0, LANES - N_EXPERTS)))
        router_b = jnp.pad(moe_router_b[i], (0, LANES - N_EXPERTS)).reshape(1, LANES)
        if i % 2 == 0:
            logw, kdir, bvec, r, v, kk, g, bonus = rwkv_front(
                xs, mods[i], norm_g[i, 0], rw_mu[j], rw_w_rkv[j], rw_w0[j], rw_w1[j], rw_w2[j], rw_a0[j],
                rw_a1[j], rw_a2[j], rw_g1[j], rw_g2[j], rw_k_k[j], rw_k_a[j], rw_r_k[j], hsel, hselt, n_ctx)
            y = wkv_scan(logw, kdir, bvec, r, v, kk, n_ctx)
            xs, h2, logits = rwkv_readout(xs, mods[i], y, bonus, g, rw_ln_g[j], rw_ln_b[j], rw_w_o[j],
                                          hsel, hselt, norm_g[i, 1], router_w, router_b)
        else:
            xs, h2, logits = conv_mixer(xs, mods[i], norm_g[i, 0], sc_w_in[j], sc_conv[j], sc_w_out[j],
                                        norm_g[i, 1], router_w, router_b, n_ctx)
        f = moe_ffn(h2.reshape(bsz * length, dm), logits.reshape(bsz * length, LANES),
                    moe_w_gu[i], moe_b_gu[i], moe_w_down[i], moe_b_down[i])
        xs = xs + gate2(i) * f.reshape(bsz, length, dm)
    return final_norm(xs, final_norm_g, n_ctx)
```

```python
import functools
import math

import jax
import jax.numpy as jnp
from jax import lax
from jax.experimental import pallas as pl
from jax.experimental.pallas import tpu as pltpu

F32 = jnp.float32
BF16 = jnp.bfloat16

HEAD = 64
CHUNK = 64
GROUP = 2
GL = GROUP * HEAD
LANES = 128
GRID_W = 64
TOKEN_TILE = 256
RMS_EPS = 1e-6
GN_EPS = HEAD * 1e-5
N_EXPERTS = 32
TOP_K = 4
N_MOD = 6
SWIGLU_LIMIT = 7.0
SWIGLU_ALPHA = 1.702
EXPERT_BLOCK = 256
V7X_VMEM_LIMIT = 56 * 1024 * 1024
HI = lax.Precision.HIGHEST
SCAN_EXACT = (False, False, False)

NN = ((1,), (0,))
NT = ((1,), (1,))
TN = ((0,), (0,))


def _dot(a, b, dims=NN, exact=False):
    dn = (dims, ((), ()))
    if exact:
        return lax.dot_general(a.astype(F32), b.astype(F32), dn, precision=HI, preferred_element_type=F32)
    return lax.dot_general(a.astype(BF16), b.astype(BF16), dn, preferred_element_type=F32)


def _sigmoid(x):
    return 1.0 / (1.0 + jnp.exp(-x))


def _params(n_grid_dims):
    return pltpu.CompilerParams(dimension_semantics=("arbitrary",) * n_grid_dims,
                                vmem_limit_bytes=V7X_VMEM_LIMIT)


def _const_spec(shape):
    nd = len(shape)
    return pl.BlockSpec(tuple(shape), lambda *_: (0,) * nd, pipeline_mode=pl.Buffered(1))


def _ada_kernel(c_ref, w_ref, b_ref, o_ref):
    cnd = c_ref[...]
    o_ref[...] = _dot(cnd * _sigmoid(cnd), w_ref[...], NN, True) + b_ref[...]


def ada_mods(cond, ada_w, ada_b):
    depth, dm, _ = ada_w.shape
    rows = cond.shape[0]
    out = pl.pallas_call(
        _ada_kernel,
        grid=(depth, N_MOD),
        in_specs=[pl.BlockSpec((rows, dm), lambda i, n: (0, 0)),
                  pl.BlockSpec((None, dm, dm), lambda i, n: (i, 0, n)),
                  pl.BlockSpec((None, 1, dm), lambda i, n: (i, 0, n))],
        out_specs=pl.BlockSpec((None, rows, dm), lambda i, n: (i, 0, n)),
        out_shape=jax.ShapeDtypeStruct((depth, rows, N_MOD * dm), F32),
        compiler_params=_params(2),
        name="ada_mods",
    )(cond, ada_w, ada_b.reshape(depth, 1, N_MOD * dm))
    return out.reshape(depth, rows, N_MOD, dm)


def _modulate(x, g, shift, scale):
    ms = jnp.mean(x * x, axis=-1, keepdims=True)
    return x * lax.rsqrt(ms + RMS_EPS) * g * (1.0 + scale) + shift


def _row_neighbours(h, row_len):
    tm = h.shape[0]
    pos = lax.broadcasted_iota(jnp.int32, (tm, 1), 0) & (row_len - 1)
    prev = jnp.where(pos != 0, pltpu.roll(h, 1, axis=0), 0.0)
    nxt = jnp.where(pos != row_len - 1, pltpu.roll(h, tm - 1, axis=0), 0.0)
    return prev, nxt


def _tile_row_len(n_ctx):
    return jnp.where(pl.program_id(1) == 0, n_ctx, GRID_W)


def _head_sum(z, hsel, hsel_t):
    zh = z.astype(BF16)
    zl = (z - zh.astype(F32)).astype(BF16)
    s = _dot(zh, hsel) + _dot(zl, hsel)
    sh = s.astype(BF16)
    sl = (s - sh.astype(F32)).astype(BF16)
    return _dot(sh, hsel_t) + _dot(sl, hsel_t)


def _moe_prologue(xn, mod_ref, ng2_ref, rw_ref, rb_ref, h2_ref, lg_ref):
    h2 = _modulate(xn, ng2_ref[...], mod_ref[3:4, :], mod_ref[4:5, :])
    h2_ref[...] = h2.astype(BF16)
    lg_ref[...] = _dot(h2, rw_ref[...]) + rb_ref[...]


def _tile_specs(dm):
    tile = pl.BlockSpec((None, TOKEN_TILE, dm), lambda b, t: (b, t, 0))
    ctx_row = lambda b, t: (jnp.where(t == 0, 4, b), 0, 0)
    mod = pl.BlockSpec((None, N_MOD, dm), ctx_row)
    return tile, mod


def _rwkv_front_kernel(x_ref, mod_ref, ng_ref, mu_ref, wrkv_ref, w1_ref, w2_ref, w0_ref, a1_ref,
                       a2_ref, a0_ref, g1_ref, g2_ref, kkw_ref, kaw_ref, rk_ref, hsel_ref, hselt_ref,
                       logw_ref, kd_ref, bv_ref, r_ref, v_ref, kk_ref, g_ref, bonus_ref, *, n_ctx):
    h = _modulate(x_ref[...], ng_ref[...], mod_ref[0:1, :], mod_ref[1:2, :])
    prev, nxt = _row_neighbours(h, _tile_row_len(n_ctx))
    d_prev = prev - h
    d_next = nxt - h

    def lerp(j):
        return h + mu_ref[2 * j:2 * j + 1, :] * d_prev + mu_ref[2 * j + 1:2 * j + 2, :] * d_next

    r = _dot(lerp(0), wrkv_ref[0])
    k = _dot(lerp(1), wrkv_ref[1])
    v = _dot(lerp(2), wrkv_ref[2])
    hw = jnp.tanh(_dot(lerp(3), w1_ref[...]))
    ha = _dot(lerp(4), a1_ref[...])
    g_ref[...] = _dot(_sigmoid(_dot(lerp(5), g1_ref[...])), g2_ref[...])

    hsel, hselt = hsel_ref[...], hselt_ref[...]
    kkraw = k * kkw_ref[...]
    kk = kkraw / jnp.maximum(jnp.sqrt(_head_sum(kkraw * kkraw, hsel, hselt)), 1e-12)
    ksum = jnp.zeros_like(k)
    for e in range(2):
        w_log = w0_ref[e:e + 1, :] + _dot(hw, w2_ref[e])
        logw_ref[e] = -math.exp(-0.5) * _sigmoid(w_log)
        a = _sigmoid(a0_ref[e:e + 1, :] + _dot(ha, a2_ref[e]))
        kd = k * (1.0 + (a - 1.0) * kaw_ref[...])
        kd_ref[e] = kd
        bv_ref[e] = kk * a
        ksum = ksum + kd
    r_ref[...] = r
    v_ref[...] = v
    kk_ref[...] = kk
    bonus_ref[...] = _head_sum(r * ksum * rk_ref[...], hsel, hselt) * v


def rwkv_front(xs, mods, norm_g, mu, w_rkv, w0, w1, w2, a0, a1, a2, g1, g2, k_k, k_a, r_k, hsel, hselt,
               n_ctx):
    bsz, length, dm = xs.shape
    lora_w, lora_a = w1.shape[-1], a1.shape[-1]
    tile, mod = _tile_specs(dm)
    dir_tile = pl.BlockSpec((2, None, TOKEN_TILE, dm), lambda b, t: (0, b, t, 0))

    def pad_rows(w, e):
        z = jnp.zeros_like(w[e])
        return jnp.concatenate([w[0] if e == 0 else z, w[1] if e == 1 else z], axis=0)

    w1c = jnp.concatenate([w1[0], w1[1]], axis=1).astype(BF16)
    a1c = jnp.concatenate([a1[0], a1[1]], axis=1).astype(BF16)
    w2p = jnp.stack([pad_rows(w2, 0), pad_rows(w2, 1)]).astype(BF16)
    a2p = jnp.stack([pad_rows(a2, 0), pad_rows(a2, 1)]).astype(BF16)
    vec = lambda a: a.reshape(1, dm)
    consts = [vec(norm_g), mu.reshape(12, dm), w_rkv.astype(BF16), w1c, w2p, w0, a1c, a2p, a0,
              g1.astype(BF16), g2.astype(BF16), vec(k_k), vec(k_a), vec(r_k), hsel, hselt]
    sds = jax.ShapeDtypeStruct
    return pl.pallas_call(
        functools.partial(_rwkv_front_kernel, n_ctx=n_ctx),
        grid=(bsz, length // TOKEN_TILE),
        in_specs=[tile, mod] + [_const_spec(a.shape) for a in consts],
        out_specs=[dir_tile, dir_tile, dir_tile, tile, tile, tile, tile, tile],
        out_shape=[sds((2, bsz, length, dm), F32)] * 3 + [sds((bsz, length, dm), F32)] * 5,
        compiler_params=_params(2),
        name="rwkv_front",
    )(xs, mods, *consts)


def _wkv_kernel(lwf_ref, kdf_ref, bf_ref, rf_ref, vf_ref, kkf_ref, lwb_ref, kdb_ref, bb_ref, rb_ref, vb_ref,
                kkb_ref, yf_ref, yb_ref, s_ref, *, exact):
    ex_a, ex_t, ex_s = exact
    n_groups = s_ref.shape[1]

    @pl.when(pl.program_id(1) == 0)
    def _():
        s_ref[...] = jnp.zeros_like(s_ref)

    row = lax.broadcasted_iota(jnp.int32, (CHUNK, CHUNK), 0)
    col = lax.broadcasted_iota(jnp.int32, (CHUNK, CHUNK), 1)
    rowc = lax.broadcasted_iota(jnp.int32, (CHUNK, GL), 0)
    colc = lax.broadcasted_iota(jnp.int32, (CHUNK, GL), 1) % HEAD
    brow = lax.broadcasted_iota(jnp.int32, (GL, GL), 0) // HEAD
    bcol = lax.broadcasted_iota(jnp.int32, (GL, GL), 1) // HEAD
    bmask = brow == bcol
    eye = rowc == colc

    def bd(y):
        return jnp.where(bmask, jnp.concatenate([y] * GROUP, axis=0), 0.0)

    def bdmm(x, y):
        return _dot(x, bd(y), NN, ex_t)

    st = []
    for d, (lw_ref, kd_ref, b_ref, r_ref, v_ref, kk_ref) in enumerate(
            ((lwf_ref, kdf_ref, bf_ref, rf_ref, vf_ref, kkf_ref),
             (lwb_ref, kdb_ref, bb_ref, rb_ref, vb_ref, kkb_ref))):
        sgn = 1 - 2 * d
        tri = ((row - col) * sgn >= 0).astype(F32)
        dcat = (rowc - colc) * sgn
        strict, incl = dcat > 0, dcat >= 0
        lw = lw_ref[...]
        cum = _dot(tri, lw, NN, True)
        e_tot = jnp.exp(jnp.sum(lw, axis=0, keepdims=True))
        e_neg = jnp.exp(-cum)
        kkp = kk_ref[...] * jnp.exp(cum - lw)
        rp = r_ref[...] * jnp.exp(cum)
        kinv = kd_ref[...] * e_neg
        binv = b_ref[...] * e_neg
        kdec = kinv * e_tot
        bdec = binv * e_tot
        v = v_ref[...]
        for g in range(n_groups):
            sl = slice(g * GL, (g + 1) * GL)
            st.append(dict(d=d, g=g, sl=sl, strict=strict, incl=incl, v=v[:, sl], kinv=kinv[:, sl],
                           binv=binv[:, sl], e_tot=e_tot[:, sl],
                           lhs2=jnp.concatenate([kkp[:, sl], rp[:, sl]], axis=0),
                           dec=jnp.concatenate([kdec[:, sl], -bdec[:, sl]], axis=0)))
    n = range(len(st))
    ak = [_dot(st[i]["lhs2"], bd(st[i]["kinv"]), NT, ex_a) for i in n]
    ab = [_dot(st[i]["lhs2"], bd(st[i]["binv"]), NT, ex_a) for i in n]
    a_kr = [jnp.concatenate([jnp.where(st[i]["strict"], ak[i][:CHUNK], 0.0),
                             jnp.where(st[i]["incl"], ak[i][CHUNK:], 0.0)], axis=0) for i in n]
    arb = [jnp.where(st[i]["incl"], ab[i][CHUNK:], 0.0) for i in n]
    mp = [-jnp.where(st[i]["strict"], ab[i][:CHUNK], 0.0) for i in n]
    t = [jnp.where(eye, 1.0, 0.0) + mp[i] for i in n]
    mp = [bdmm(mp[i], mp[i]) for i in n]
    m = 2
    while 2 * m < CHUNK:
        r2 = [bdmm(jnp.concatenate([mp[i], t[i]], axis=0), mp[i]) for i in n]
        mp = [r2[i][:CHUNK] for i in n]
        t = [t[i] + r2[i][CHUNK:] for i in n]
        m *= 2
    t = [t[i] + bdmm(t[i], mp[i]) for i in n]
    av = [bdmm(a_kr[i], st[i]["v"]) for i in n]
    s = [s_ref[st[i]["d"], st[i]["g"]] for i in n]
    zs = [_dot(st[i]["lhs2"], s[i], NT, ex_s) for i in n]
    u = [bdmm(t[i], zs[i][:CHUNK] + av[i][:CHUNK]) for i in n]
    for i in n:
        y_ref = yb_ref if st[i]["d"] else yf_ref
        y_ref[:, st[i]["sl"]] = zs[i][CHUNK:] + av[i][CHUNK:] - bdmm(arb[i], u[i])
    upd = [_dot(jnp.concatenate([st[i]["v"], u[i]], axis=0), st[i]["dec"], TN, ex_s) for i in n]
    for i in n:
        s_ref[st[i]["d"], st[i]["g"]] = s[i] * st[i]["e_tot"] + jnp.where(bmask, upd[i], 0.0)


def wkv_scan(logw, kdir, bvec, r, v, kk, n_ctx, exact=SCAN_EXACT):
    _, bsz, length, dm = logw.shape
    nc = length // CHUNK
    ncc = n_ctx // CHUNK

    def back(c):
        return jnp.where(c < ncc, ncc - 1 - c, nc - 1 + ncc - c)

    dir_f = pl.BlockSpec((None, None, CHUNK, dm), lambda b, c: (0, b, c, 0))
    dir_b = pl.BlockSpec((None, None, CHUNK, dm), lambda b, c: (1, b, back(c), 0))
    sh_f = pl.BlockSpec((None, CHUNK, dm), lambda b, c: (b, c, 0))
    sh_b = pl.BlockSpec((None, CHUNK, dm), lambda b, c: (b, back(c), 0))
    sds = jax.ShapeDtypeStruct((bsz, length, dm), F32)
    return pl.pallas_call(
        functools.partial(_wkv_kernel, exact=exact),
        grid=(bsz, nc),
        in_specs=[dir_f, dir_f, dir_f, sh_f, sh_f, sh_f, dir_b, dir_b, dir_b, sh_b, sh_b, sh_b],
        out_specs=[sh_f, sh_b],
        out_shape=[sds, sds],
        scratch_shapes=[pltpu.VMEM((2, dm // GL, GL, GL), F32)],
        compiler_params=_params(2),
        name="wkv_scan",
    )(logw, kdir, bvec, r, v, kk, logw, kdir, bvec, r, v, kk)


def _rwkv_readout_kernel(x_ref, mod_ref, yf_ref, yb_ref, bonus_ref, g_ref, lng_ref, lnb_ref, wo_ref, hsel_ref,
                         hselt_ref, ng2_ref, rw_ref, rb_ref, xo_ref, h2_ref, lg_ref):
    hsel, hselt = hsel_ref[...], hselt_ref[...]
    y = yf_ref[...] + yb_ref[...]
    yc = y - _head_sum(y, hsel, hselt) * (1.0 / HEAD)
    var = _head_sum(yc * yc, hsel, hselt) * (1.0 / HEAD)
    yn = yc * lax.rsqrt(var + GN_EPS) * lng_ref[...] + lnb_ref[...]
    z = (yn + bonus_ref[...]) * g_ref[...]
    xn = x_ref[...] + mod_ref[2:3, :] * _dot(z, wo_ref[...])
    xo_ref[...] = xn
    _moe_prologue(xn, mod_ref, ng2_ref, rw_ref, rb_ref, h2_ref, lg_ref)


def rwkv_readout(xs, mods, y_fwd, y_bwd, bonus, g, ln_g, ln_b, w_o, hsel, hselt, ng2, router_w, router_b):
    bsz, length, dm = xs.shape
    tile, mod = _tile_specs(dm)
    lg_tile = pl.BlockSpec((None, TOKEN_TILE, LANES), lambda b, t: (b, t, 0))
    vec = lambda a: a.reshape(1, -1)
    consts = [vec(ln_g), vec(ln_b), w_o.astype(BF16), hsel, hselt, vec(ng2), router_w, router_b]
    sds = jax.ShapeDtypeStruct
    return pl.pallas_call(
        _rwkv_readout_kernel,
        grid=(bsz, length // TOKEN_TILE),
        in_specs=[tile, mod, tile, tile, tile, tile] + [_const_spec(a.shape) for a in consts],
        out_specs=[tile, tile, lg_tile],
        out_shape=[sds((bsz, length, dm), F32), sds((bsz, length, dm), BF16),
                   sds((bsz, length, LANES), F32)],
        compiler_params=_params(2),
        name="rwkv_readout",
    )(xs, mods, y_fwd, y_bwd, bonus, g, *consts)


def _conv_kernel(x_ref, mod_ref, ng_ref, win_ref, cw_ref, wout_ref, ng2_ref, rw_ref, rb_ref,
                 xo_ref, h2_ref, lg_ref, *, n_ctx):
    x = x_ref[...]
    dm = x.shape[1]
    h = _modulate(x, ng_ref[...], mod_ref[0:1, :], mod_ref[1:2, :])
    hw = _dot(h, win_ref[...])
    b_gate, c_gate, u = hw[:, :dm], hw[:, dm:2 * dm], hw[:, 2 * dm:]
    z = c_gate * u
    z_prev, z_next = _row_neighbours(z, _tile_row_len(n_ctx))
    z = cw_ref[0:1, :] * z_prev + cw_ref[1:2, :] * z + cw_ref[2:3, :] * z_next
    xn = x + mod_ref[2:3, :] * _dot(b_gate * z, wout_ref[...])
    xo_ref[...] = xn
    _moe_prologue(xn, mod_ref, ng2_ref, rw_ref, rb_ref, h2_ref, lg_ref)


def conv_mixer(xs, mods, ng1, w_in, conv_w, w_out, ng2, router_w, router_b, n_ctx):
    bsz, length, dm = xs.shape
    tile, mod = _tile_specs(dm)
    lg_tile = pl.BlockSpec((None, TOKEN_TILE, LANES), lambda b, t: (b, t, 0))
    vec = lambda a: a.reshape(1, -1)
    consts = [vec(ng1), w_in.astype(BF16), conv_w, w_out.astype(BF16), vec(ng2), router_w, router_b]
    sds = jax.ShapeDtypeStruct
    return pl.pallas_call(
        functools.partial(_conv_kernel, n_ctx=n_ctx),
        grid=(bsz, length // TOKEN_TILE),
        in_specs=[tile, mod] + [_const_spec(a.shape) for a in consts],
        out_specs=[tile, tile, lg_tile],
        out_shape=[sds((bsz, length, dm), F32), sds((bsz, length, dm), BF16),
                   sds((bsz, length, LANES), F32)],
        compiler_params=_params(2),
        name="conv_mixer",
    )(xs, mods, *consts)


def _expert_kernel(be_ref, xb_ref, wgu_ref, bgu_ref, wd_ref, bdn_ref, o_ref, wgu_bf, wd_bf):
    i = pl.program_id(0)
    new_expert = jnp.logical_or(i == 0, be_ref[i] != be_ref[jnp.maximum(i - 1, 0)])

    @pl.when(new_expert)
    def _():
        wgu_bf[...] = wgu_ref[...].astype(BF16)
        wd_bf[...] = wd_ref[...].astype(BF16)

    hgu = _dot(xb_ref[...], wgu_bf[...]) + bgu_ref[...]
    ff = hgu.shape[1] // 2
    g_ = jnp.minimum(hgu[:, :ff], SWIGLU_LIMIT)
    u_ = jnp.clip(hgu[:, ff:], -SWIGLU_LIMIT, SWIGLU_LIMIT)
    act = (u_ + 1.0) * (g_ * _sigmoid(SWIGLU_ALPHA * g_))
    o_ref[...] = _dot(act, wd_bf[...]) + bdn_ref[...]


def expert_ffn(block_e, xb, layer, w_gu, b_gu, w_down, b_down):
    n_rows, dm = xb.shape
    depth, n_e, _, ff2 = w_gu.shape
    grid_spec = pltpu.PrefetchScalarGridSpec(
        num_scalar_prefetch=1,
        grid=(n_rows // EXPERT_BLOCK,),
        in_specs=[pl.BlockSpec((EXPERT_BLOCK, dm), lambda i, be: (i, 0)),
                  pl.BlockSpec((None, None, dm, ff2), lambda i, be: (layer, be[i], 0, 0)),
                  pl.BlockSpec((None, None, 1, ff2), lambda i, be: (layer, be[i], 0, 0)),
                  pl.BlockSpec((None, None, ff2 // 2, dm), lambda i, be: (layer, be[i], 0, 0)),
                  pl.BlockSpec((None, None, 1, dm), lambda i, be: (layer, be[i], 0, 0))],
        out_specs=pl.BlockSpec((EXPERT_BLOCK, dm), lambda i, be: (i, 0)),
        scratch_shapes=[pltpu.VMEM((dm, ff2), BF16), pltpu.VMEM((ff2 // 2, dm), BF16)])
    return pl.pallas_call(
        _expert_kernel,
        grid_spec=grid_spec,
        out_shape=jax.ShapeDtypeStruct((n_rows, dm), F32),
        compiler_params=_params(1),
        name="expert_ffn",
    )(block_e, xb, w_gu, b_gu.reshape(depth, n_e, 1, ff2), w_down, b_down.reshape(depth, n_e, 1, dm))


def moe_ffn(h2, logits, layer, w_gu, b_gu, w_down, b_down):
    t, dm = h2.shape
    top_logit, top_e = lax.top_k(logits[:, :N_EXPERTS], TOP_K)
    gate = jax.nn.softmax(top_logit, axis=-1)
    expert = top_e.reshape(-1).astype(jnp.int32)
    n_assign = t * TOP_K
    eids = jnp.arange(N_EXPERTS, dtype=jnp.int32)
    counts = jnp.sum((expert[:, None] == eids[None, :]).astype(jnp.int32), axis=0)
    padded = (counts + EXPERT_BLOCK - 1) // EXPERT_BLOCK * EXPERT_BLOCK
    start = jnp.cumsum(counts) - counts
    pad_end = jnp.cumsum(padded)
    pad_start = pad_end - padded
    order = jnp.argsort(expert)
    rank = jnp.argsort(order).astype(jnp.int32)
    dest = pad_start[expert] + rank - start[expert]
    n_blocks = -(-(n_assign + N_EXPERTS * (EXPERT_BLOCK - 1)) // EXPERT_BLOCK)
    n_rows = n_blocks * EXPERT_BLOCK
    block_start = jnp.arange(n_blocks, dtype=jnp.int32) * EXPERT_BLOCK
    block_e = jnp.minimum(jnp.sum((pad_end[None, :] <= block_start[:, None]).astype(jnp.int32), axis=1),
                          N_EXPERTS - 1)
    row_e = jnp.repeat(block_e, EXPERT_BLOCK)
    row_q = jnp.arange(n_rows, dtype=jnp.int32) - pad_start[row_e]
    row_valid = row_q < counts[row_e]
    row_src = jnp.where(row_valid, start[row_e] + row_q, 0)
    row_tok = jnp.where(row_valid, order[row_src] // TOP_K, 0)
    yb = expert_ffn(block_e, h2[row_tok], layer, w_gu, b_gu, w_down, b_down)
    dest_km = dest.reshape(t, TOP_K).T
    f = yb[dest_km[0]] * gate[:, 0:1]
    for k in range(1, TOP_K):
        f = f + yb[dest_km[k]] * gate[:, k:k + 1]
    return f


def _final_norm_kernel(x_ref, g_ref, o_ref):
    x = x_ref[...]
    ms = jnp.mean(x * x, axis=-1, keepdims=True)
    o_ref[...] = x * lax.rsqrt(ms + RMS_EPS) * g_ref[...]


def final_norm(xs, g, n_ctx):
    bsz, length, dm = xs.shape
    skip = n_ctx // TOKEN_TILE
    return pl.pallas_call(
        _final_norm_kernel,
        grid=(bsz, length // TOKEN_TILE - skip),
        in_specs=[pl.BlockSpec((None, TOKEN_TILE, dm), lambda b, t: (b, t + skip, 0)),
                  pl.BlockSpec((1, dm), lambda b, t: (0, 0))],
        out_specs=pl.BlockSpec((None, TOKEN_TILE, dm), lambda b, t: (b, t, 0)),
        out_shape=jax.ShapeDtypeStruct((bsz, length - n_ctx, dm), F32),
        compiler_params=_params(2),
        name="final_norm",
    )(xs, g.reshape(1, dm))


def kernel(x, c, ctx, c_ctx, ada_w, ada_b, norm_g, final_norm_g, rw_mu, rw_w_rkv, rw_w0, rw_w1, rw_w2, rw_a0, rw_a1, rw_a2, rw_g1, rw_g2, rw_k_k, rw_k_a, rw_r_k, rw_ln_g, rw_ln_b, rw_w_o, sc_w_in, sc_conv, sc_w_out, moe_router_w, moe_router_b, moe_w_gu, moe_b_gu, moe_w_down, moe_b_down):
    bsz, seq, dm = x.shape
    n_ctx = ctx.shape[1]
    depth = ada_w.shape[0]
    assert n_ctx == TOKEN_TILE and seq % TOKEN_TILE == 0 and dm % GL == 0 and bsz == 4
    length = n_ctx + seq

    cond = jnp.concatenate([c, c_ctx[None], jnp.zeros((3, dm), F32)], axis=0)
    mods = ada_mods(cond, ada_w, ada_b)
    xs = jnp.concatenate([ctx, x], axis=1)

    head_of_lane = jnp.arange(dm, dtype=jnp.int32) // HEAD
    hsel = (head_of_lane[:, None] == jnp.arange(LANES, dtype=jnp.int32)[None, :]).astype(BF16)
    hselt = hsel.T
    gate2 = lambda i: jnp.concatenate(
        [jnp.broadcast_to(mods[i, 4, 5][None, None], (bsz, n_ctx, dm)),
         jnp.broadcast_to(mods[i, :bsz, 5][:, None], (bsz, seq, dm))], axis=1)

    for i in range(depth):
        j = i // 2
        router_w = jnp.pad(moe_router_w[i], ((0, 0), (0, LANES - N_EXPERTS))).astype(BF16)
        router_b = jnp.pad(moe_router_b[i], (0, LANES - N_EXPERTS)).reshape(1, LANES)
        if i % 2 == 0:
            logw, kdir, bvec, r, v, kk, g, bonus = rwkv_front(
                xs, mods[i], norm_g[i, 0], rw_mu[j], rw_w_rkv[j], rw_w0[j], rw_w1[j], rw_w2[j], rw_a0[j],
                rw_a1[j], rw_a2[j], rw_g1[j], rw_g2[j], rw_k_k[j], rw_k_a[j], rw_r_k[j], hsel, hselt, n_ctx)
            y_fwd, y_bwd = wkv_scan(logw, kdir, bvec, r, v, kk, n_ctx)
            xs, h2, logits = rwkv_readout(xs, mods[i], y_fwd, y_bwd, bonus, g, rw_ln_g[j], rw_ln_b[j], rw_w_o[j],
                                          hsel, hselt, norm_g[i, 1], router_w, router_b)
        else:
            xs, h2, logits = conv_mixer(xs, mods[i], norm_g[i, 0], sc_w_in[j], sc_conv[j], sc_w_out[j],
                                        norm_g[i, 1], router_w, router_b, n_ctx)
        f = moe_ffn(h2.reshape(bsz * length, dm), logits.reshape(bsz * length, LANES),
                    i, moe_w_gu, moe_b_gu, moe_w_down, moe_b_down)
        xs = xs + gate2(i) * f.reshape(bsz, length, dm)
    return final_norm(xs, final_norm_g, n_ctx)
```

```python
import functools
import math

import jax
import jax.numpy as jnp
from jax import lax
from jax.experimental import pallas as pl
from jax.experimental.pallas import tpu as pltpu

F32 = jnp.float32
BF16 = jnp.bfloat16

HEAD = 64
CHUNK = 64
GROUP = 2
GL = GROUP * HEAD
LANES = 128
GRID_W = 64
TOKEN_TILE = 256
RMS_EPS = 1e-6
GN_EPS = HEAD * 1e-5
N_EXPERTS = 32
TOP_K = 4
N_MOD = 6
SWIGLU_LIMIT = 7.0
SWIGLU_ALPHA = 1.702
EXPERT_BLOCK = 256
V7X_VMEM_LIMIT = 56 * 1024 * 1024
HI = lax.Precision.HIGHEST
SCAN_EXACT = (False, False, False)

NN = ((1,), (0,))
NT = ((1,), (1,))
TN = ((0,), (0,))


def _dot(a, b, dims=NN, exact=False):
    dn = (dims, ((), ()))
    if exact:
        return lax.dot_general(a.astype(F32), b.astype(F32), dn, precision=HI, preferred_element_type=F32)
    return lax.dot_general(a.astype(BF16), b.astype(BF16), dn, preferred_element_type=F32)


def _sigmoid(x):
    return 1.0 / (1.0 + jnp.exp(-x))


def _params(n_grid_dims):
    return pltpu.CompilerParams(dimension_semantics=("arbitrary",) * n_grid_dims,
                                vmem_limit_bytes=V7X_VMEM_LIMIT)


def _const_spec(shape):
    nd = len(shape)
    return pl.BlockSpec(tuple(shape), lambda *_: (0,) * nd, pipeline_mode=pl.Buffered(1))


def _ada_kernel(c_ref, w_ref, b_ref, o_ref):
    cnd = c_ref[...]
    o_ref[...] = _dot(cnd * _sigmoid(cnd), w_ref[...], NN, True) + b_ref[...]


def ada_mods(cond, ada_w, ada_b):
    depth, dm, _ = ada_w.shape
    rows = cond.shape[0]
    out = pl.pallas_call(
        _ada_kernel,
        grid=(depth, N_MOD),
        in_specs=[pl.BlockSpec((rows, dm), lambda i, n: (0, 0)),
                  pl.BlockSpec((None, dm, dm), lambda i, n: (i, 0, n)),
                  pl.BlockSpec((None, 1, dm), lambda i, n: (i, 0, n))],
        out_specs=pl.BlockSpec((None, rows, dm), lambda i, n: (i, 0, n)),
        out_shape=jax.ShapeDtypeStruct((depth, rows, N_MOD * dm), F32),
        compiler_params=_params(2),
        name="ada_mods",
    )(cond, ada_w, ada_b.reshape(depth, 1, N_MOD * dm))
    return out.reshape(depth, rows, N_MOD, dm)


def _modulate(x, g, shift, scale):
    ms = jnp.mean(x * x, axis=-1, keepdims=True)
    return x * lax.rsqrt(ms + RMS_EPS) * g * (1.0 + scale) + shift


def _row_neighbours(h, row_len):
    tm = h.shape[0]
    pos = lax.broadcasted_iota(jnp.int32, (tm, 1), 0) & (row_len - 1)
    prev = jnp.where(pos != 0, pltpu.roll(h, 1, axis=0), 0.0)
    nxt = jnp.where(pos != row_len - 1, pltpu.roll(h, tm - 1, axis=0), 0.0)
    return prev, nxt


def _tile_row_len(n_ctx):
    return jnp.where(pl.program_id(1) == 0, n_ctx, GRID_W)


def _head_sum(z, hsel, hsel_t):
    zh = z.astype(BF16)
    zl = (z - zh.astype(F32)).astype(BF16)
    s = _dot(zh, hsel) + _dot(zl, hsel)
    sh = s.astype(BF16)
    sl = (s - sh.astype(F32)).astype(BF16)
    return _dot(sh, hsel_t) + _dot(sl, hsel_t)


def _moe_prologue(xn, mod_ref, ng2_ref, rw_ref, rb_ref, h2_ref, lg_ref):
    h2 = _modulate(xn, ng2_ref[...], mod_ref[3:4, :], mod_ref[4:5, :])
    h2_ref[...] = h2
    lg_ref[...] = _dot(h2, rw_ref[...]) + rb_ref[...]


def _tile_specs(dm):
    tile = pl.BlockSpec((None, TOKEN_TILE, dm), lambda b, t: (b, t, 0))
    ctx_row = lambda b, t: (jnp.where(t == 0, 4, b), 0, 0)
    mod = pl.BlockSpec((None, N_MOD, dm), ctx_row)
    return tile, mod


N_FFN_REFS = TOP_K + 2


def _ffn_specs(dm, skip=0):
    rows = pl.BlockSpec((None, TOKEN_TILE, dm), lambda b, t: (b, t + skip, 0))
    gate = pl.BlockSpec((None, TOKEN_TILE, TOP_K), lambda b, t: (b, t + skip, 0))
    mod = pl.BlockSpec((None, N_MOD, dm), lambda b, t: (jnp.where(t + skip == 0, 4, b), 0, 0))
    return [rows] * TOP_K + [gate, mod]


def _add_ffn(x, ffn_refs):
    *row_refs, gate_ref, mod_ref = ffn_refs
    f = row_refs[0][...] * gate_ref[:, 0:1]
    for k in range(1, TOP_K):
        f = f + row_refs[k][...] * gate_ref[:, k:k + 1]
    return x + mod_ref[5:6, :] * f


def _rwkv_front_kernel(x_ref, *refs, n_ctx, n_ffn):
    ffn_refs, refs = refs[:n_ffn], refs[n_ffn:]
    (mod_ref, ng_ref, mu_ref, wrkv_ref, w1_ref, w2_ref, w0_ref, a1_ref, a2_ref, a0_ref, g1_ref, g2_ref,
     kkw_ref, kaw_ref, rk_ref, hsel_ref, hselt_ref,
     logw_ref, kd_ref, bv_ref, r_ref, v_ref, kk_ref, g_ref, bonus_ref, *xo_ref) = refs
    x = x_ref[...]
    if n_ffn:
        x = _add_ffn(x, ffn_refs)
        xo_ref[0][...] = x
    h = _modulate(x, ng_ref[...], mod_ref[0:1, :], mod_ref[1:2, :])
    prev, nxt = _row_neighbours(h, _tile_row_len(n_ctx))
    d_prev = prev - h
    d_next = nxt - h

    def lerp(j):
        return h + mu_ref[2 * j:2 * j + 1, :] * d_prev + mu_ref[2 * j + 1:2 * j + 2, :] * d_next

    r = _dot(lerp(0), wrkv_ref[0])
    k = _dot(lerp(1), wrkv_ref[1])
    v = _dot(lerp(2), wrkv_ref[2])
    hw = jnp.tanh(_dot(lerp(3), w1_ref[...]))
    ha = _dot(lerp(4), a1_ref[...])
    g_ref[...] = _dot(_sigmoid(_dot(lerp(5), g1_ref[...])), g2_ref[...])

    hsel, hselt = hsel_ref[...], hselt_ref[...]
    kkraw = k * kkw_ref[...]
    kk = kkraw / jnp.maximum(jnp.sqrt(_head_sum(kkraw * kkraw, hsel, hselt)), 1e-12)
    ksum = jnp.zeros_like(k)
    for e in range(2):
        w_log = w0_ref[e:e + 1, :] + _dot(hw, w2_ref[e])
        logw_ref[e] = -math.exp(-0.5) * _sigmoid(w_log)
        a = _sigmoid(a0_ref[e:e + 1, :] + _dot(ha, a2_ref[e]))
        kd = k * (1.0 + (a - 1.0) * kaw_ref[...])
        kd_ref[e] = kd
        bv_ref[e] = kk * a
        ksum = ksum + kd
    r_ref[...] = r
    v_ref[...] = v
    kk_ref[...] = kk
    bonus_ref[...] = _head_sum(r * ksum * rk_ref[...], hsel, hselt) * v


def rwkv_front(xs, ffn, mods, norm_g, mu, w_rkv, w0, w1, w2, a0, a1, a2, g1, g2, k_k, k_a, r_k, hsel, hselt,
               n_ctx):
    bsz, length, dm = xs.shape
    lora_w, lora_a = w1.shape[-1], a1.shape[-1]
    tile, mod = _tile_specs(dm)
    dir_tile = pl.BlockSpec((2, None, TOKEN_TILE, dm), lambda b, t: (0, b, t, 0))

    def pad_rows(w, e):
        z = jnp.zeros_like(w[e])
        return jnp.concatenate([w[0] if e == 0 else z, w[1] if e == 1 else z], axis=0)

    w1c = jnp.concatenate([w1[0], w1[1]], axis=1).astype(BF16)
    a1c = jnp.concatenate([a1[0], a1[1]], axis=1).astype(BF16)
    w2p = jnp.stack([pad_rows(w2, 0), pad_rows(w2, 1)]).astype(BF16)
    a2p = jnp.stack([pad_rows(a2, 0), pad_rows(a2, 1)]).astype(BF16)
    vec = lambda a: a.reshape(1, dm)
    consts = [vec(norm_g), mu.reshape(12, dm), w_rkv.astype(BF16), w1c, w2p, w0, a1c, a2p, a0,
              g1.astype(BF16), g2.astype(BF16), vec(k_k), vec(k_a), vec(r_k), hsel, hselt]
    sds = jax.ShapeDtypeStruct
    return pl.pallas_call(
        functools.partial(_rwkv_front_kernel, n_ctx=n_ctx, n_ffn=len(ffn)),
        grid=(bsz, length // TOKEN_TILE),
        in_specs=[tile] + (_ffn_specs(dm) if ffn else []) + [mod] + [_const_spec(a.shape) for a in consts],
        out_specs=[dir_tile, dir_tile, dir_tile] + [tile] * (6 if ffn else 5),
        out_shape=[sds((2, bsz, length, dm), F32)] * 3 + [sds((bsz, length, dm), F32)] * (6 if ffn else 5),
        compiler_params=_params(2),
        name="rwkv_front",
    )(xs, *ffn, mods, *consts)


def _wkv_kernel(lwf_ref, kdf_ref, bf_ref, rf_ref, vf_ref, kkf_ref, lwb_ref, kdb_ref, bb_ref, rb_ref, vb_ref,
                kkb_ref, yf_ref, yb_ref, s_ref, *, exact):
    ex_a, ex_t, ex_s = exact
    n_groups = s_ref.shape[1]

    @pl.when(pl.program_id(1) == 0)
    def _():
        s_ref[...] = jnp.zeros_like(s_ref)

    row = lax.broadcasted_iota(jnp.int32, (CHUNK, CHUNK), 0)
    col = lax.broadcasted_iota(jnp.int32, (CHUNK, CHUNK), 1)
    rowc = lax.broadcasted_iota(jnp.int32, (CHUNK, GL), 0)
    colc = lax.broadcasted_iota(jnp.int32, (CHUNK, GL), 1) % HEAD
    brow = lax.broadcasted_iota(jnp.int32, (GL, GL), 0) // HEAD
    bcol = lax.broadcasted_iota(jnp.int32, (GL, GL), 1) // HEAD
    bmask = brow == bcol
    eye = rowc == colc

    def bd(y):
        return jnp.where(bmask, jnp.concatenate([y] * GROUP, axis=0), 0.0)

    def bdmm(x, y):
        return _dot(x, bd(y), NN, ex_t)

    st = []
    for d, (lw_ref, kd_ref, b_ref, r_ref, v_ref, kk_ref) in enumerate(
            ((lwf_ref, kdf_ref, bf_ref, rf_ref, vf_ref, kkf_ref),
             (lwb_ref, kdb_ref, bb_ref, rb_ref, vb_ref, kkb_ref))):
        sgn = 1 - 2 * d
        tri = ((row - col) * sgn >= 0).astype(F32)
        dcat = (rowc - colc) * sgn
        strict, incl = dcat > 0, dcat >= 0
        lw = lw_ref[...]
        lw1 = lw.astype(BF16)
        rem = lw - lw1.astype(F32)
        lw2 = rem.astype(BF16)
        lw3 = (rem - lw2.astype(F32)).astype(BF16)
        cum = _dot(jnp.concatenate([tri] * 3, axis=1), jnp.concatenate([lw1, lw2, lw3], axis=0))
        e_tot = jnp.exp(jnp.sum(lw, axis=0, keepdims=True))
        e_neg = jnp.exp(-cum)
        kkp = kk_ref[...] * jnp.exp(cum - lw)
        rp = r_ref[...] * jnp.exp(cum)
        kinv = kd_ref[...] * e_neg
        binv = b_ref[...] * e_neg
        kdec = kinv * e_tot
        bdec = binv * e_tot
        v = v_ref[...]
        for g in range(n_groups):
            sl = slice(g * GL, (g + 1) * GL)
            st.append(dict(d=d, g=g, sl=sl, strict=strict, incl=incl, v=v[:, sl], kinv=kinv[:, sl],
                           binv=binv[:, sl], e_tot=e_tot[:, sl],
                           lhs2=jnp.concatenate([kkp[:, sl], rp[:, sl]], axis=0),
                           dec=jnp.concatenate([kdec[:, sl], -bdec[:, sl]], axis=0)))
    n = range(len(st))
    akb = [_dot(st[i]["lhs2"], jnp.concatenate([bd(st[i]["kinv"]), bd(st[i]["binv"])], axis=0), NT, ex_a)
           for i in n]
    ak = [akb[i][:, :GL] for i in n]
    ab = [akb[i][:, GL:] for i in n]
    a_kr = [jnp.concatenate([jnp.where(st[i]["strict"], ak[i][:CHUNK], 0.0),
                             jnp.where(st[i]["incl"], ak[i][CHUNK:], 0.0)], axis=0) for i in n]
    arb = [jnp.where(st[i]["incl"], ab[i][CHUNK:], 0.0) for i in n]
    mp = [-jnp.where(st[i]["strict"], ab[i][:CHUNK], 0.0) for i in n]
    t = [jnp.where(eye, 1.0, 0.0) + mp[i] for i in n]
    mp = [bdmm(mp[i], mp[i]) for i in n]
    m = 2
    while 2 * m < CHUNK:
        r2 = [bdmm(jnp.concatenate([mp[i], t[i]], axis=0), mp[i]) for i in n]
        mp = [r2[i][:CHUNK] for i in n]
        t = [t[i] + r2[i][CHUNK:] for i in n]
        m *= 2
    t = [t[i] + bdmm(t[i], mp[i]) for i in n]
    av = [bdmm(a_kr[i], st[i]["v"]) for i in n]
    s = [s_ref[st[i]["d"], st[i]["g"]] for i in n]
    zs = [_dot(st[i]["lhs2"], s[i], NT, ex_s) for i in n]
    u = [bdmm(t[i], zs[i][:CHUNK] + av[i][:CHUNK]) for i in n]
    for i in n:
        y_ref = yb_ref if st[i]["d"] else yf_ref
        y_ref[:, st[i]["sl"]] = zs[i][CHUNK:] + av[i][CHUNK:] - bdmm(arb[i], u[i])
    upd = [_dot(jnp.concatenate([st[i]["v"], u[i]], axis=0), st[i]["dec"], TN, ex_s) for i in n]
    for i in n:
        s_ref[st[i]["d"], st[i]["g"]] = s[i] * st[i]["e_tot"] + jnp.where(bmask, upd[i], 0.0)


def wkv_scan(logw, kdir, bvec, r, v, kk, n_ctx, exact=SCAN_EXACT):
    _, bsz, length, dm = logw.shape
    nc = length // CHUNK
    ncc = n_ctx // CHUNK

    def back(c):
        return jnp.where(c < ncc, ncc - 1 - c, nc - 1 + ncc - c)

    dir_f = pl.BlockSpec((None, None, CHUNK, dm), lambda b, c: (0, b, c, 0))
    dir_b = pl.BlockSpec((None, None, CHUNK, dm), lambda b, c: (1, b, back(c), 0))
    sh_f = pl.BlockSpec((None, CHUNK, dm), lambda b, c: (b, c, 0))
    sh_b = pl.BlockSpec((None, CHUNK, dm), lambda b, c: (b, back(c), 0))
    sds = jax.ShapeDtypeStruct((bsz, length, dm), F32)
    return pl.pallas_call(
        functools.partial(_wkv_kernel, exact=exact),
        grid=(bsz, nc),
        in_specs=[dir_f, dir_f, dir_f, sh_f, sh_f, sh_f, dir_b, dir_b, dir_b, sh_b, sh_b, sh_b],
        out_specs=[sh_f, sh_b],
        out_shape=[sds, sds],
        scratch_shapes=[pltpu.VMEM((2, dm // GL, GL, GL), F32)],
        compiler_params=_params(2),
        name="wkv_scan",
    )(logw, kdir, bvec, r, v, kk, logw, kdir, bvec, r, v, kk)


def _rwkv_readout_kernel(x_ref, mod_ref, yf_ref, yb_ref, bonus_ref, g_ref, lng_ref, lnb_ref, wo_ref, hsel_ref,
                         hselt_ref, ng2_ref, rw_ref, rb_ref, xo_ref, h2_ref, lg_ref):
    hsel, hselt = hsel_ref[...], hselt_ref[...]
    y = yf_ref[...] + yb_ref[...]
    yc = y - _head_sum(y, hsel, hselt) * (1.0 / HEAD)
    var = _head_sum(yc * yc, hsel, hselt) * (1.0 / HEAD)
    yn = yc * lax.rsqrt(var + GN_EPS) * lng_ref[...] + lnb_ref[...]
    z = (yn + bonus_ref[...]) * g_ref[...]
    xn = x_ref[...] + mod_ref[2:3, :] * _dot(z, wo_ref[...])
    xo_ref[...] = xn
    _moe_prologue(xn, mod_ref, ng2_ref, rw_ref, rb_ref, h2_ref, lg_ref)


def rwkv_readout(xs, mods, y_fwd, y_bwd, bonus, g, ln_g, ln_b, w_o, hsel, hselt, ng2, router_w, router_b):
    bsz, length, dm = xs.shape
    tile, mod = _tile_specs(dm)
    lg_tile = pl.BlockSpec((None, TOKEN_TILE, LANES), lambda b, t: (b, t, 0))
    vec = lambda a: a.reshape(1, -1)
    consts = [vec(ln_g), vec(ln_b), w_o.astype(BF16), hsel, hselt, vec(ng2), router_w, router_b]
    sds = jax.ShapeDtypeStruct
    return pl.pallas_call(
        _rwkv_readout_kernel,
        grid=(bsz, length // TOKEN_TILE),
        in_specs=[tile, mod, tile, tile, tile, tile] + [_const_spec(a.shape) for a in consts],
        out_specs=[tile, tile, lg_tile],
        out_shape=[sds((bsz, length, dm), F32), sds((bsz, length, dm), F32),
                   sds((bsz, length, LANES), F32)],
        compiler_params=_params(2),
        name="rwkv_readout",
    )(xs, mods, y_fwd, y_bwd, bonus, g, *consts)


def _conv_kernel(x_ref, *refs, n_ctx, n_ffn):
    ffn_refs, refs = refs[:n_ffn], refs[n_ffn:]
    mod_ref, ng_ref, win_ref, cw_ref, wout_ref, ng2_ref, rw_ref, rb_ref, xo_ref, h2_ref, lg_ref = refs
    x = x_ref[...]
    if n_ffn:
        x = _add_ffn(x, ffn_refs)
    dm = x.shape[1]
    h = _modulate(x, ng_ref[...], mod_ref[0:1, :], mod_ref[1:2, :])
    hw = _dot(h, win_ref[...])
    b_gate, c_gate, u = hw[:, :dm], hw[:, dm:2 * dm], hw[:, 2 * dm:]
    z = c_gate * u
    z_prev, z_next = _row_neighbours(z, _tile_row_len(n_ctx))
    z = cw_ref[0:1, :] * z_prev + cw_ref[1:2, :] * z + cw_ref[2:3, :] * z_next
    xn = x + mod_ref[2:3, :] * _dot(b_gate * z, wout_ref[...])
    xo_ref[...] = xn
    _moe_prologue(xn, mod_ref, ng2_ref, rw_ref, rb_ref, h2_ref, lg_ref)


def conv_mixer(xs, ffn, mods, ng1, w_in, conv_w, w_out, ng2, router_w, router_b, n_ctx):
    bsz, length, dm = xs.shape
    tile, mod = _tile_specs(dm)
    lg_tile = pl.BlockSpec((None, TOKEN_TILE, LANES), lambda b, t: (b, t, 0))
    vec = lambda a: a.reshape(1, -1)
    consts = [vec(ng1), w_in.astype(BF16), conv_w, w_out.astype(BF16), vec(ng2), router_w, router_b]
    sds = jax.ShapeDtypeStruct
    return pl.pallas_call(
        functools.partial(_conv_kernel, n_ctx=n_ctx, n_ffn=len(ffn)),
        grid=(bsz, length // TOKEN_TILE),
        in_specs=[tile] + (_ffn_specs(dm) if ffn else []) + [mod] + [_const_spec(a.shape) for a in consts],
        out_specs=[tile, tile, lg_tile],
        out_shape=[sds((bsz, length, dm), F32), sds((bsz, length, dm), F32),
                   sds((bsz, length, LANES), F32)],
        compiler_params=_params(2),
        name="conv_mixer",
    )(xs, *ffn, mods, *consts)


def _expert_kernel(be_ref, xb_ref, wgu_ref, bgu_ref, wd_ref, bdn_ref, o_ref, wgu_bf, wd_bf):
    i = pl.program_id(0)
    new_expert = jnp.logical_or(i == 0, be_ref[i] != be_ref[jnp.maximum(i - 1, 0)])

    @pl.when(new_expert)
    def _():
        wgu_bf[...] = wgu_ref[...].astype(BF16)
        wd_bf[...] = wd_ref[...].astype(BF16)

    hgu = _dot(xb_ref[...], wgu_bf[...]) + bgu_ref[...]
    ff = hgu.shape[1] // 2
    g_ = jnp.minimum(hgu[:, :ff], SWIGLU_LIMIT)
    u_ = jnp.clip(hgu[:, ff:], -SWIGLU_LIMIT, SWIGLU_LIMIT)
    act = (u_ + 1.0) * (g_ * _sigmoid(SWIGLU_ALPHA * g_))
    o_ref[...] = _dot(act, wd_bf[...]) + bdn_ref[...]


def expert_ffn(block_e, xb, layer, w_gu, b_gu, w_down, b_down):
    n_rows, dm = xb.shape
    depth, n_e, _, ff2 = w_gu.shape
    grid_spec = pltpu.PrefetchScalarGridSpec(
        num_scalar_prefetch=1,
        grid=(n_rows // EXPERT_BLOCK,),
        in_specs=[pl.BlockSpec((EXPERT_BLOCK, dm), lambda i, be: (i, 0)),
                  pl.BlockSpec((None, None, dm, ff2), lambda i, be: (layer, be[i], 0, 0)),
                  pl.BlockSpec((None, None, 1, ff2), lambda i, be: (layer, be[i], 0, 0)),
                  pl.BlockSpec((None, None, ff2 // 2, dm), lambda i, be: (layer, be[i], 0, 0)),
                  pl.BlockSpec((None, None, 1, dm), lambda i, be: (layer, be[i], 0, 0))],
        out_specs=pl.BlockSpec((EXPERT_BLOCK, dm), lambda i, be: (i, 0)),
        scratch_shapes=[pltpu.VMEM((dm, ff2), BF16), pltpu.VMEM((ff2 // 2, dm), BF16)])
    return pl.pallas_call(
        _expert_kernel,
        grid_spec=grid_spec,
        out_shape=jax.ShapeDtypeStruct((n_rows, dm), F32),
        compiler_params=_params(1),
        name="expert_ffn",
    )(block_e, xb, w_gu, b_gu.reshape(depth, n_e, 1, ff2), w_down, b_down.reshape(depth, n_e, 1, dm))


def moe_ffn(h2, logits, layer, w_gu, b_gu, w_down, b_down):
    t, dm = h2.shape
    top_logit, top_e = lax.top_k(logits[:, :N_EXPERTS], TOP_K)
    gate = jax.nn.softmax(top_logit, axis=-1)
    expert = top_e.reshape(-1).astype(jnp.int32)
    n_assign = t * TOP_K
    eids = jnp.arange(N_EXPERTS, dtype=jnp.int32)
    onehot = (expert[:, None] == eids[None, :]).astype(jnp.int32)
    csum = jnp.cumsum(onehot, axis=0)
    counts = csum[-1]
    padded = (counts + EXPERT_BLOCK - 1) // EXPERT_BLOCK * EXPERT_BLOCK
    pad_end = jnp.cumsum(padded)
    pad_start = pad_end - padded
    dest = jnp.sum(onehot * (csum - 1 + pad_start[None, :]), axis=1)
    n_blocks = -(-(n_assign + N_EXPERTS * (EXPERT_BLOCK - 1)) // EXPERT_BLOCK)
    n_rows = n_blocks * EXPERT_BLOCK
    block_start = jnp.arange(n_blocks, dtype=jnp.int32) * EXPERT_BLOCK
    block_e = jnp.minimum(jnp.sum((pad_end[None, :] <= block_start[:, None]).astype(jnp.int32), axis=1),
                          N_EXPERTS - 1)
    token = jnp.arange(n_assign, dtype=jnp.int32) // TOP_K
    row_tok = jnp.zeros((n_rows,), jnp.int32).at[dest].add(token, unique_indices=True)
    yb = expert_ffn(block_e, h2[row_tok], layer, w_gu, b_gu, w_down, b_down)
    dest_km = dest.reshape(t, TOP_K).T
    return [yb[dest_km[k]] for k in range(TOP_K)], gate


def _final_norm_kernel(x_ref, *refs):
    *ffn_refs, g_ref, o_ref = refs
    x = _add_ffn(x_ref[...], ffn_refs)
    ms = jnp.mean(x * x, axis=-1, keepdims=True)
    o_ref[...] = x * lax.rsqrt(ms + RMS_EPS) * g_ref[...]


def final_norm(xs, ffn, g, n_ctx):
    bsz, length, dm = xs.shape
    skip = n_ctx // TOKEN_TILE
    return pl.pallas_call(
        _final_norm_kernel,
        grid=(bsz, length // TOKEN_TILE - skip),
        in_specs=[pl.BlockSpec((None, TOKEN_TILE, dm), lambda b, t: (b, t + skip, 0))]
        + _ffn_specs(dm, skip) + [pl.BlockSpec((1, dm), lambda b, t: (0, 0))],
        out_specs=pl.BlockSpec((None, TOKEN_TILE, dm), lambda b, t: (b, t, 0)),
        out_shape=jax.ShapeDtypeStruct((bsz, length - n_ctx, dm), F32),
        compiler_params=_params(2),
        name="final_norm",
    )(xs, *ffn, g.reshape(1, dm))


def kernel(x, c, ctx, c_ctx, ada_w, ada_b, norm_g, final_norm_g, rw_mu, rw_w_rkv, rw_w0, rw_w1, rw_w2, rw_a0, rw_a1, rw_a2, rw_g1, rw_g2, rw_k_k, rw_k_a, rw_r_k, rw_ln_g, rw_ln_b, rw_w_o, sc_w_in, sc_conv, sc_w_out, moe_router_w, moe_router_b, moe_w_gu, moe_b_gu, moe_w_down, moe_b_down):
    bsz, seq, dm = x.shape
    n_ctx = ctx.shape[1]
    depth = ada_w.shape[0]
    assert n_ctx == TOKEN_TILE and seq % TOKEN_TILE == 0 and dm % GL == 0 and bsz == 4
    length = n_ctx + seq

    cond = jnp.concatenate([c, c_ctx[None], jnp.zeros((3, dm), F32)], axis=0)
    mods = ada_mods(cond, ada_w, ada_b)
    xs = jnp.concatenate([ctx, x], axis=1)

    head_of_lane = jnp.arange(dm, dtype=jnp.int32) // HEAD
    hsel = (head_of_lane[:, None] == jnp.arange(LANES, dtype=jnp.int32)[None, :]).astype(BF16)
    hselt = hsel.T
    ffn = ()
    for i in range(depth):
        j = i // 2
        router_w = jnp.pad(moe_router_w[i], ((0, 0), (0, LANES - N_EXPERTS))).astype(BF16)
        router_b = jnp.pad(moe_router_b[i], (0, LANES - N_EXPERTS)).reshape(1, LANES)
        if i % 2 == 0:
            logw, kdir, bvec, r, v, kk, g, bonus, *x_new = rwkv_front(
                xs, ffn, mods[i], norm_g[i, 0], rw_mu[j], rw_w_rkv[j], rw_w0[j], rw_w1[j], rw_w2[j], rw_a0[j],
                rw_a1[j], rw_a2[j], rw_g1[j], rw_g2[j], rw_k_k[j], rw_k_a[j], rw_r_k[j], hsel, hselt, n_ctx)
            xs = x_new[0] if ffn else xs
            y_fwd, y_bwd = wkv_scan(logw, kdir, bvec, r, v, kk, n_ctx)
            xs, h2, logits = rwkv_readout(xs, mods[i], y_fwd, y_bwd, bonus, g, rw_ln_g[j], rw_ln_b[j], rw_w_o[j],
                                          hsel, hselt, norm_g[i, 1], router_w, router_b)
        else:
            xs, h2, logits = conv_mixer(xs, ffn, mods[i], norm_g[i, 0], sc_w_in[j], sc_conv[j], sc_w_out[j],
                                        norm_g[i, 1], router_w, router_b, n_ctx)
        rows, gate = moe_ffn(h2.reshape(bsz * length, dm), logits.reshape(bsz * length, LANES),
                             i, moe_w_gu, moe_b_gu, moe_w_down, moe_b_down)
        ffn = tuple(a.reshape(bsz, length, dm) for a in rows) + (gate.reshape(bsz, length, TOP_K), mods[i])
    return final_norm(xs, ffn, final_norm_g, n_ctx)
```

```python
import functools
import math

import jax
import jax.numpy as jnp
from jax import lax
from jax.experimental import pallas as pl
from jax.experimental.pallas import tpu as pltpu

F32 = jnp.float32
BF16 = jnp.bfloat16

HEAD = 64
CHUNK = 64
GROUP = 2
GL = GROUP * HEAD
LANES = 128
GRID_W = 64
TOKEN_TILE = 256
RMS_EPS = 1e-6
GN_EPS = HEAD * 1e-5
N_EXPERTS = 32
TOP_K = 4
N_MOD = 6
SWIGLU_LIMIT = 7.0
SWIGLU_ALPHA = 1.702
EXPERT_BLOCK = 512
V7X_VMEM_LIMIT = 56 * 1024 * 1024
HI = lax.Precision.HIGHEST
SCAN_SUB = 2
SCAN_EXACT = (False, False, False)

NN = ((1,), (0,))
NT = ((1,), (1,))
TN = ((0,), (0,))


def _dot(a, b, dims=NN, exact=False):
    dn = (dims, ((), ()))
    if exact:
        return lax.dot_general(a.astype(F32), b.astype(F32), dn, precision=HI, preferred_element_type=F32)
    return lax.dot_general(a.astype(BF16), b.astype(BF16), dn, preferred_element_type=F32)


def _sigmoid(x):
    return 0.5 * jnp.tanh(0.5 * x) + 0.5


def _params(n_grid_dims):
    return pltpu.CompilerParams(dimension_semantics=("arbitrary",) * n_grid_dims,
                                vmem_limit_bytes=V7X_VMEM_LIMIT)


def _const_spec(shape):
    nd = len(shape)
    return pl.BlockSpec(tuple(shape), lambda *_: (0,) * nd, pipeline_mode=pl.Buffered(1))


def _ada_kernel(c_ref, w_ref, b_ref, o_ref):
    cnd = c_ref[...]
    o_ref[...] = _dot(cnd * _sigmoid(cnd), w_ref[...], NN, True) + b_ref[...]


def ada_mods(cond, ada_w, ada_b):
    depth, dm, _ = ada_w.shape
    rows = cond.shape[0]
    out = pl.pallas_call(
        _ada_kernel,
        grid=(depth, N_MOD),
        in_specs=[pl.BlockSpec((rows, dm), lambda i, n: (0, 0)),
                  pl.BlockSpec((None, dm, dm), lambda i, n: (i, 0, n)),
                  pl.BlockSpec((None, 1, dm), lambda i, n: (i, 0, n))],
        out_specs=pl.BlockSpec((None, rows, dm), lambda i, n: (i, 0, n)),
        out_shape=jax.ShapeDtypeStruct((depth, rows, N_MOD * dm), F32),
        compiler_params=_params(2),
        name="ada_mods",
    )(cond, ada_w, ada_b.reshape(depth, 1, N_MOD * dm))
    return out.reshape(depth, rows, N_MOD, dm)


def _modulate(x, g, shift, scale):
    ms = jnp.mean(x * x, axis=-1, keepdims=True)
    return x * lax.rsqrt(ms + RMS_EPS) * g * (1.0 + scale) + shift


def _row_neighbours(h, row_len):
    tm = h.shape[0]
    pos = lax.broadcasted_iota(jnp.int32, (tm, 1), 0) & (row_len - 1)
    prev = jnp.where(pos != 0, pltpu.roll(h, 1, axis=0), 0.0)
    nxt = jnp.where(pos != row_len - 1, pltpu.roll(h, tm - 1, axis=0), 0.0)
    return prev, nxt


def _tile_row_len(n_ctx):
    return jnp.where(pl.program_id(1) == 0, n_ctx, GRID_W)


def _head_sum(z, hsel, hsel_t):
    zh = z.astype(BF16)
    zl = (z - zh.astype(F32)).astype(BF16)
    s = _dot(zh, hsel) + _dot(zl, hsel)
    sh = s.astype(BF16)
    sl = (s - sh.astype(F32)).astype(BF16)
    return _dot(sh, hsel_t) + _dot(sl, hsel_t)


def _moe_prologue(xn, mod_ref, ng2_ref, rw_ref, rb_ref, h2_ref, lg_ref):
    h2 = _modulate(xn, ng2_ref[...], mod_ref[3:4, :], mod_ref[4:5, :])
    h2_ref[...] = h2
    lg_ref[...] = _dot(h2, rw_ref[...]) + rb_ref[...]


def _tile_specs(dm):
    tile = pl.BlockSpec((None, TOKEN_TILE, dm), lambda b, t: (b, t, 0))
    ctx_row = lambda b, t: (jnp.where(t == 0, 4, b), 0, 0)
    mod = pl.BlockSpec((None, N_MOD, dm), ctx_row)
    return tile, mod


N_FFN_REFS = TOP_K + 2


def _ffn_specs(dm, skip=0):
    rows = pl.BlockSpec((None, TOKEN_TILE, dm), lambda b, t: (b, t + skip, 0))
    gate = pl.BlockSpec((None, TOKEN_TILE, TOP_K), lambda b, t: (b, t + skip, 0))
    mod = pl.BlockSpec((None, N_MOD, dm), lambda b, t: (jnp.where(t + skip == 0, 4, b), 0, 0))
    return [rows] * TOP_K + [gate, mod]


def _add_ffn(x, ffn_refs):
    *row_refs, gate_ref, mod_ref = ffn_refs
    f = row_refs[0][...] * gate_ref[:, 0:1]
    for k in range(1, TOP_K):
        f = f + row_refs[k][...] * gate_ref[:, k:k + 1]
    return x + mod_ref[5:6, :] * f


def _rwkv_front_kernel(x_ref, *refs, n_ctx, n_ffn):
    ffn_refs, refs = refs[:n_ffn], refs[n_ffn:]
    (mod_ref, ng_ref, mu_ref, wrkv_ref, w1_ref, w2_ref, w0_ref, a1_ref, a2_ref, a0_ref, g1_ref, g2_ref,
     kkw_ref, kaw_ref, rk_ref, hsel_ref, hselt_ref,
     logw_ref, kd_ref, bv_ref, r_ref, v_ref, kk_ref, g_ref, bonus_ref, *xo_ref) = refs
    x = x_ref[...]
    if n_ffn:
        x = _add_ffn(x, ffn_refs)
        xo_ref[0][...] = x
    h = _modulate(x, ng_ref[...], mod_ref[0:1, :], mod_ref[1:2, :])
    prev, nxt = _row_neighbours(h, _tile_row_len(n_ctx))
    hb = h.astype(BF16)
    d_prev = (prev - h).astype(BF16)
    d_next = (nxt - h).astype(BF16)

    def lerp(j):
        return hb + mu_ref[2 * j:2 * j + 1, :] * d_prev + mu_ref[2 * j + 1:2 * j + 2, :] * d_next

    r = _dot(lerp(0), wrkv_ref[0])
    k = _dot(lerp(1), wrkv_ref[1])
    v = _dot(lerp(2), wrkv_ref[2])
    hw = jnp.tanh(_dot(lerp(3), w1_ref[...]))
    ha = _dot(lerp(4), a1_ref[...])
    g_ref[...] = _dot(_sigmoid(_dot(lerp(5), g1_ref[...])), g2_ref[...])

    hsel, hselt = hsel_ref[...], hselt_ref[...]
    kkraw = k * kkw_ref[...]
    kk = kkraw / jnp.maximum(jnp.sqrt(_head_sum(kkraw * kkraw, hsel, hselt)), 1e-12)
    ksum = jnp.zeros_like(k)
    for e in range(2):
        w_log = w0_ref[e:e + 1, :] + _dot(hw, w2_ref[e])
        logw_ref[e] = -math.exp(-0.5) * _sigmoid(w_log)
        a = _sigmoid(a0_ref[e:e + 1, :] + _dot(ha, a2_ref[e]))
        kd = k * (1.0 + (a - 1.0) * kaw_ref[...])
        kd_ref[e] = kd
        bv_ref[e] = kk * a
        ksum = ksum + kd
    r_ref[...] = r
    v_ref[...] = v
    kk_ref[...] = kk
    bonus_ref[...] = _head_sum(r * ksum * rk_ref[...], hsel, hselt) * v


def rwkv_front(xs, ffn, mods, norm_g, mu, w_rkv, w0, w1, w2, a0, a1, a2, g1, g2, k_k, k_a, r_k, hsel, hselt,
               n_ctx):
    bsz, length, dm = xs.shape
    lora_w, lora_a = w1.shape[-1], a1.shape[-1]
    tile, mod = _tile_specs(dm)
    dir_tile = pl.BlockSpec((2, None, TOKEN_TILE, dm), lambda b, t: (0, b, t, 0))

    def pad_rows(w, e):
        z = jnp.zeros_like(w[e])
        return jnp.concatenate([w[0] if e == 0 else z, w[1] if e == 1 else z], axis=0)

    w1c = jnp.concatenate([w1[0], w1[1]], axis=1).astype(BF16)
    a1c = jnp.concatenate([a1[0], a1[1]], axis=1).astype(BF16)
    w2p = jnp.stack([pad_rows(w2, 0), pad_rows(w2, 1)]).astype(BF16)
    a2p = jnp.stack([pad_rows(a2, 0), pad_rows(a2, 1)]).astype(BF16)
    vec = lambda a: a.reshape(1, dm)
    consts = [vec(norm_g), mu.reshape(12, dm).astype(BF16), w_rkv.astype(BF16), w1c, w2p, w0, a1c, a2p, a0,
              g1.astype(BF16), g2.astype(BF16), vec(k_k), vec(k_a), vec(r_k), hsel, hselt]
    sds = jax.ShapeDtypeStruct
    return pl.pallas_call(
        functools.partial(_rwkv_front_kernel, n_ctx=n_ctx, n_ffn=len(ffn)),
        grid=(bsz, length // TOKEN_TILE),
        in_specs=[tile] + (_ffn_specs(dm) if ffn else []) + [mod] + [_const_spec(a.shape) for a in consts],
        out_specs=[dir_tile, dir_tile, dir_tile] + [tile] * (6 if ffn else 5),
        out_shape=[sds((2, bsz, length, dm), F32)] * 3 + [sds((bsz, length, dm), F32)] * (6 if ffn else 5),
        compiler_params=_params(2),
        name="rwkv_front",
    )(xs, *ffn, mods, *consts)


def _wkv_kernel(*refs, exact):
    s_ref = refs[-1]

    @pl.when(pl.program_id(1) == 0)
    def _():
        s_ref[...] = jnp.zeros_like(s_ref)

    for sub in range(SCAN_SUB):
        _wkv_chunk_pair(sub, *refs, exact=exact)


def _wkv_chunk_pair(sub, lwf_ref, kdf_ref, bf_ref, rf_ref, vf_ref, kkf_ref, lwb_ref, kdb_ref, bb_ref, rb_ref,
                    vb_ref, kkb_ref, yf_ref, yb_ref, s_ref, *, exact):
    ex_a, ex_t, ex_s = exact
    n_groups = s_ref.shape[1]
    row = lax.broadcasted_iota(jnp.int32, (CHUNK, CHUNK), 0)
    col = lax.broadcasted_iota(jnp.int32, (CHUNK, CHUNK), 1)
    rowc = lax.broadcasted_iota(jnp.int32, (CHUNK, GL), 0)
    colc = lax.broadcasted_iota(jnp.int32, (CHUNK, GL), 1) % HEAD
    brow = lax.broadcasted_iota(jnp.int32, (GL, GL), 0) // HEAD
    bcol = lax.broadcasted_iota(jnp.int32, (GL, GL), 1) // HEAD
    bmask = brow == bcol
    eye = rowc == colc

    def bd(y):
        return jnp.where(bmask, jnp.concatenate([y] * GROUP, axis=0), 0.0)

    def bdmm(x, y):
        return _dot(x, bd(y), NN, ex_t)

    st = []
    for d, (lw_ref, kd_ref, b_ref, r_ref, v_ref, kk_ref) in enumerate(
            ((lwf_ref, kdf_ref, bf_ref, rf_ref, vf_ref, kkf_ref),
             (lwb_ref, kdb_ref, bb_ref, rb_ref, vb_ref, kkb_ref))):
        sgn = 1 - 2 * d
        tri = ((row - col) * sgn >= 0).astype(F32)
        dcat = (rowc - colc) * sgn
        strict, incl = dcat > 0, dcat >= 0
        first = (sub if d == 0 else SCAN_SUB - 1 - sub) * CHUNK
        rows = slice(first, first + CHUNK)
        lw = lw_ref[rows, :]
        lw1 = lw.astype(BF16)
        rem = lw - lw1.astype(F32)
        lw2 = rem.astype(BF16)
        lw3 = (rem - lw2.astype(F32)).astype(BF16)
        cum = _dot(jnp.concatenate([tri] * 3, axis=1), jnp.concatenate([lw1, lw2, lw3], axis=0))
        e_tot = jnp.exp(jnp.sum(lw, axis=0, keepdims=True))
        e_neg = jnp.exp(-cum)
        kkp = kk_ref[rows, :] * jnp.exp(cum - lw)
        rp = r_ref[rows, :] * jnp.exp(cum)
        kinv = kd_ref[rows, :] * e_neg
        binv = b_ref[rows, :] * e_neg
        kdec = kinv * e_tot
        bdec = binv * e_tot
        v = v_ref[rows, :]
        for g in range(n_groups):
            sl = slice(g * GL, (g + 1) * GL)
            st.append(dict(d=d, g=g, sl=sl, rows=rows, strict=strict, incl=incl, v=v[:, sl], kinv=kinv[:, sl],
                           binv=binv[:, sl], e_tot=e_tot[:, sl],
                           lhs2=jnp.concatenate([kkp[:, sl], rp[:, sl]], axis=0),
                           dec=jnp.concatenate([kdec[:, sl], -bdec[:, sl]], axis=0)))
    n = range(len(st))
    akb = [_dot(st[i]["lhs2"], jnp.concatenate([bd(st[i]["kinv"]), bd(st[i]["binv"])], axis=0), NT, ex_a)
           for i in n]
    ak = [akb[i][:, :GL] for i in n]
    ab = [akb[i][:, GL:] for i in n]
    a_kr = [jnp.concatenate([jnp.where(st[i]["strict"], ak[i][:CHUNK], 0.0),
                             jnp.where(st[i]["incl"], ak[i][CHUNK:], 0.0)], axis=0) for i in n]
    arb = [jnp.where(st[i]["incl"], ab[i][CHUNK:], 0.0) for i in n]
    mp = [-jnp.where(st[i]["strict"], ab[i][:CHUNK], 0.0) for i in n]
    t = [jnp.where(eye, 1.0, 0.0) + mp[i] for i in n]
    mp = [bdmm(mp[i], mp[i]) for i in n]
    m = 2
    while 2 * m < CHUNK:
        r2 = [bdmm(jnp.concatenate([mp[i], t[i]], axis=0), mp[i]) for i in n]
        mp = [r2[i][:CHUNK] for i in n]
        t = [t[i] + r2[i][CHUNK:] for i in n]
        m *= 2
    t = [t[i] + bdmm(t[i], mp[i]) for i in n]
    av = [bdmm(a_kr[i], st[i]["v"]) for i in n]
    s = [s_ref[st[i]["d"], st[i]["g"]] for i in n]
    zs = [_dot(st[i]["lhs2"], s[i], NT, ex_s) for i in n]
    u = [bdmm(t[i], zs[i][:CHUNK] + av[i][:CHUNK]) for i in n]
    for i in n:
        y_ref = yb_ref if st[i]["d"] else yf_ref
        y_ref[st[i]["rows"], st[i]["sl"]] = zs[i][CHUNK:] + av[i][CHUNK:] - bdmm(arb[i], u[i])
    upd = [_dot(jnp.concatenate([st[i]["v"], u[i]], axis=0), st[i]["dec"], TN, ex_s) for i in n]
    for i in n:
        s_ref[st[i]["d"], st[i]["g"]] = s[i] * st[i]["e_tot"] + jnp.where(bmask, upd[i], 0.0)


def wkv_scan(logw, kdir, bvec, r, v, kk, n_ctx, exact=SCAN_EXACT):
    _, bsz, length, dm = logw.shape
    blk = SCAN_SUB * CHUNK
    assert n_ctx % blk == 0 and length % blk == 0
    nc = length // blk
    ncc = n_ctx // blk

    def back(c):
        return jnp.where(c < ncc, ncc - 1 - c, nc - 1 + ncc - c)

    dir_f = pl.BlockSpec((None, None, blk, dm), lambda b, c: (0, b, c, 0))
    dir_b = pl.BlockSpec((None, None, blk, dm), lambda b, c: (1, b, back(c), 0))
    sh_f = pl.BlockSpec((None, blk, dm), lambda b, c: (b, c, 0))
    sh_b = pl.BlockSpec((None, blk, dm), lambda b, c: (b, back(c), 0))
    sds = jax.ShapeDtypeStruct((bsz, length, dm), F32)
    return pl.pallas_call(
        functools.partial(_wkv_kernel, exact=exact),
        grid=(bsz, nc),
        in_specs=[dir_f, dir_f, dir_f, sh_f, sh_f, sh_f, dir_b, dir_b, dir_b, sh_b, sh_b, sh_b],
        out_specs=[sh_f, sh_b],
        out_shape=[sds, sds],
        scratch_shapes=[pltpu.VMEM((2, dm // GL, GL, GL), F32)],
        compiler_params=_params(2),
        name="wkv_scan",
    )(logw, kdir, bvec, r, v, kk, logw, kdir, bvec, r, v, kk)


def _rwkv_readout_kernel(x_ref, mod_ref, yf_ref, yb_ref, bonus_ref, g_ref, lng_ref, lnb_ref, wo_ref, hsel_ref,
                         hselt_ref, ng2_ref, rw_ref, rb_ref, xo_ref, h2_ref, lg_ref):
    hsel, hselt = hsel_ref[...], hselt_ref[...]
    y = yf_ref[...] + yb_ref[...]
    yc = y - _head_sum(y, hsel, hselt) * (1.0 / HEAD)
    var = _head_sum(yc * yc, hsel, hselt) * (1.0 / HEAD)
    yn = yc * lax.rsqrt(var + GN_EPS) * lng_ref[...] + lnb_ref[...]
    z = (yn + bonus_ref[...]) * g_ref[...]
    xn = x_ref[...] + mod_ref[2:3, :] * _dot(z, wo_ref[...])
    xo_ref[...] = xn
    _moe_prologue(xn, mod_ref, ng2_ref, rw_ref, rb_ref, h2_ref, lg_ref)


def rwkv_readout(xs, mods, y_fwd, y_bwd, bonus, g, ln_g, ln_b, w_o, hsel, hselt, ng2, router_w, router_b):
    bsz, length, dm = xs.shape
    tile, mod = _tile_specs(dm)
    lg_tile = pl.BlockSpec((None, TOKEN_TILE, LANES), lambda b, t: (b, t, 0))
    vec = lambda a: a.reshape(1, -1)
    consts = [vec(ln_g), vec(ln_b), w_o.astype(BF16), hsel, hselt, vec(ng2), router_w, router_b]
    sds = jax.ShapeDtypeStruct
    return pl.pallas_call(
        _rwkv_readout_kernel,
        grid=(bsz, length // TOKEN_TILE),
        in_specs=[tile, mod, tile, tile, tile, tile] + [_const_spec(a.shape) for a in consts],
        out_specs=[tile, tile, lg_tile],
        out_shape=[sds((bsz, length, dm), F32), sds((bsz, length, dm), F32),
                   sds((bsz, length, LANES), F32)],
        compiler_params=_params(2),
        name="rwkv_readout",
    )(xs, mods, y_fwd, y_bwd, bonus, g, *consts)


def _conv_kernel(x_ref, *refs, n_ctx, n_ffn):
    ffn_refs, refs = refs[:n_ffn], refs[n_ffn:]
    mod_ref, ng_ref, win_ref, cw_ref, wout_ref, ng2_ref, rw_ref, rb_ref, xo_ref, h2_ref, lg_ref = refs
    x = x_ref[...]
    if n_ffn:
        x = _add_ffn(x, ffn_refs)
    dm = x.shape[1]
    h = _modulate(x, ng_ref[...], mod_ref[0:1, :], mod_ref[1:2, :])
    hw = _dot(h, win_ref[...])
    b_gate, c_gate, u = hw[:, :dm], hw[:, dm:2 * dm], hw[:, 2 * dm:]
    z = c_gate * u
    z_prev, z_next = _row_neighbours(z, _tile_row_len(n_ctx))
    z = cw_ref[0:1, :] * z_prev + cw_ref[1:2, :] * z + cw_ref[2:3, :] * z_next
    xn = x + mod_ref[2:3, :] * _dot(b_gate * z, wout_ref[...])
    xo_ref[...] = xn
    _moe_prologue(xn, mod_ref, ng2_ref, rw_ref, rb_ref, h2_ref, lg_ref)


def conv_mixer(xs, ffn, mods, ng1, w_in, conv_w, w_out, ng2, router_w, router_b, n_ctx):
    bsz, length, dm = xs.shape
    tile, mod = _tile_specs(dm)
    lg_tile = pl.BlockSpec((None, TOKEN_TILE, LANES), lambda b, t: (b, t, 0))
    vec = lambda a: a.reshape(1, -1)
    consts = [vec(ng1), w_in.astype(BF16), conv_w, w_out.astype(BF16), vec(ng2), router_w, router_b]
    sds = jax.ShapeDtypeStruct
    return pl.pallas_call(
        functools.partial(_conv_kernel, n_ctx=n_ctx, n_ffn=len(ffn)),
        grid=(bsz, length // TOKEN_TILE),
        in_specs=[tile] + (_ffn_specs(dm) if ffn else []) + [mod] + [_const_spec(a.shape) for a in consts],
        out_specs=[tile, tile, lg_tile],
        out_shape=[sds((bsz, length, dm), F32), sds((bsz, length, dm), F32),
                   sds((bsz, length, LANES), F32)],
        compiler_params=_params(2),
        name="conv_mixer",
    )(xs, *ffn, mods, *consts)


def _expert_kernel(be_ref, xb_ref, wgu_ref, bgu_ref, wd_ref, bdn_ref, o_ref, wgu_bf, wd_bf):
    i = pl.program_id(0)
    new_expert = jnp.logical_or(i == 0, be_ref[i] != be_ref[jnp.maximum(i - 1, 0)])

    @pl.when(new_expert)
    def _():
        wgu_bf[...] = wgu_ref[...].astype(BF16)
        wd_bf[...] = wd_ref[...].astype(BF16)

    hgu = _dot(xb_ref[...], wgu_bf[...]) + bgu_ref[...]
    ff = hgu.shape[1] // 2
    g_ = jnp.minimum(hgu[:, :ff], SWIGLU_LIMIT)
    u_ = jnp.clip(hgu[:, ff:], -SWIGLU_LIMIT, SWIGLU_LIMIT)
    act = (u_ + 1.0) * (g_ * _sigmoid(SWIGLU_ALPHA * g_))
    o_ref[...] = _dot(act, wd_bf[...]) + bdn_ref[...]


def expert_ffn(block_e, xb, layer, w_gu, b_gu, w_down, b_down):
    n_rows, dm = xb.shape
    depth, n_e, _, ff2 = w_gu.shape
    grid_spec = pltpu.PrefetchScalarGridSpec(
        num_scalar_prefetch=1,
        grid=(n_rows // EXPERT_BLOCK,),
        in_specs=[pl.BlockSpec((EXPERT_BLOCK, dm), lambda i, be: (i, 0)),
                  pl.BlockSpec((None, None, dm, ff2), lambda i, be: (layer, be[i], 0, 0)),
                  pl.BlockSpec((None, None, 1, ff2), lambda i, be: (layer, be[i], 0, 0)),
                  pl.BlockSpec((None, None, ff2 // 2, dm), lambda i, be: (layer, be[i], 0, 0)),
                  pl.BlockSpec((None, None, 1, dm), lambda i, be: (layer, be[i], 0, 0))],
        out_specs=pl.BlockSpec((EXPERT_BLOCK, dm), lambda i, be: (i, 0)),
        scratch_shapes=[pltpu.VMEM((dm, ff2), BF16), pltpu.VMEM((ff2 // 2, dm), BF16)])
    return pl.pallas_call(
        _expert_kernel,
        grid_spec=grid_spec,
        out_shape=jax.ShapeDtypeStruct((n_rows, dm), F32),
        compiler_params=_params(1),
        name="expert_ffn",
    )(block_e, xb, w_gu, b_gu.reshape(depth, n_e, 1, ff2), w_down, b_down.reshape(depth, n_e, 1, dm))


def moe_ffn(h2, logits, layer, w_gu, b_gu, w_down, b_down):
    t, dm = h2.shape
    top_logit, top_e = lax.top_k(logits[:, :N_EXPERTS], TOP_K)
    gate = jax.nn.softmax(top_logit, axis=-1)
    expert = top_e.reshape(-1).astype(jnp.int32)
    n_assign = t * TOP_K
    eids = jnp.arange(N_EXPERTS, dtype=jnp.int32)
    onehot = (expert[:, None] == eids[None, :]).astype(jnp.int32)
    csum = jnp.cumsum(onehot, axis=0)
    counts = csum[-1]
    padded = (counts + EXPERT_BLOCK - 1) // EXPERT_BLOCK * EXPERT_BLOCK
    pad_end = jnp.cumsum(padded)
    pad_start = pad_end - padded
    dest = jnp.sum(onehot * (csum - 1 + pad_start[None, :]), axis=1)
    n_blocks = -(-(n_assign + N_EXPERTS * (EXPERT_BLOCK - 1)) // EXPERT_BLOCK)
    n_rows = n_blocks * EXPERT_BLOCK
    block_start = jnp.arange(n_blocks, dtype=jnp.int32) * EXPERT_BLOCK
    block_e = jnp.minimum(jnp.sum((pad_end[None, :] <= block_start[:, None]).astype(jnp.int32), axis=1),
                          N_EXPERTS - 1)
    token = jnp.arange(n_assign, dtype=jnp.int32) // TOP_K
    row_tok = jnp.zeros((n_rows,), jnp.int32).at[dest].add(token, unique_indices=True)
    yb = expert_ffn(block_e, h2[row_tok], layer, w_gu, b_gu, w_down, b_down)
    dest_km = dest.reshape(t, TOP_K).T
    return [yb[dest_km[k]] for k in range(TOP_K)], gate


def _final_norm_kernel(x_ref, *refs):
    *ffn_refs, g_ref, o_ref = refs
    x = _add_ffn(x_ref[...], ffn_refs)
    ms = jnp.mean(x * x, axis=-1, keepdims=True)
    o_ref[...] = x * lax.rsqrt(ms + RMS_EPS) * g_ref[...]


def final_norm(xs, ffn, g, n_ctx):
    bsz, length, dm = xs.shape
    skip = n_ctx // TOKEN_TILE
    return pl.pallas_call(
        _final_norm_kernel,
        grid=(bsz, length // TOKEN_TILE - skip),
        in_specs=[pl.BlockSpec((None, TOKEN_TILE, dm), lambda b, t: (b, t + skip, 0))]
        + _ffn_specs(dm, skip) + [pl.BlockSpec((1, dm), lambda b, t: (0, 0))],
        out_specs=pl.BlockSpec((None, TOKEN_TILE, dm), lambda b, t: (b, t, 0)),
        out_shape=jax.ShapeDtypeStruct((bsz, length - n_ctx, dm), F32),
        compiler_params=_params(2),
        name="final_norm",
    )(xs, *ffn, g.reshape(1, dm))


def kernel(x, c, ctx, c_ctx, ada_w, ada_b, norm_g, final_norm_g, rw_mu, rw_w_rkv, rw_w0, rw_w1, rw_w2, rw_a0, rw_a1, rw_a2, rw_g1, rw_g2, rw_k_k, rw_k_a, rw_r_k, rw_ln_g, rw_ln_b, rw_w_o, sc_w_in, sc_conv, sc_w_out, moe_router_w, moe_router_b, moe_w_gu, moe_b_gu, moe_w_down, moe_b_down):
    bsz, seq, dm = x.shape
    n_ctx = ctx.shape[1]
    depth = ada_w.shape[0]
    assert n_ctx == TOKEN_TILE and seq % TOKEN_TILE == 0 and dm % GL == 0 and bsz == 4
    length = n_ctx + seq

    cond = jnp.concatenate([c, c_ctx[None], jnp.zeros((3, dm), F32)], axis=0)
    mods = ada_mods(cond, ada_w, ada_b)
    xs = jnp.concatenate([ctx, x], axis=1)

    head_of_lane = jnp.arange(dm, dtype=jnp.int32) // HEAD
    hsel = (head_of_lane[:, None] == jnp.arange(LANES, dtype=jnp.int32)[None, :]).astype(BF16)
    hselt = hsel.T
    ffn = ()
    for i in range(depth):
        j = i // 2
        router_w = jnp.pad(moe_router_w[i], ((0, 0), (0, LANES - N_EXPERTS))).astype(BF16)
        router_b = jnp.pad(moe_router_b[i], (0, LANES - N_EXPERTS)).reshape(1, LANES)
        if i % 2 == 0:
            logw, kdir, bvec, r, v, kk, g, bonus, *x_new = rwkv_front(
                xs, ffn, mods[i], norm_g[i, 0], rw_mu[j], rw_w_rkv[j], rw_w0[j], rw_w1[j], rw_w2[j], rw_a0[j],
                rw_a1[j], rw_a2[j], rw_g1[j], rw_g2[j], rw_k_k[j], rw_k_a[j], rw_r_k[j], hsel, hselt, n_ctx)
            xs = x_new[0] if ffn else xs
            y_fwd, y_bwd = wkv_scan(logw, kdir, bvec, r, v, kk, n_ctx)
            xs, h2, logits = rwkv_readout(xs, mods[i], y_fwd, y_bwd, bonus, g, rw_ln_g[j], rw_ln_b[j], rw_w_o[j],
                                          hsel, hselt, norm_g[i, 1], router_w, router_b)
        else:
            xs, h2, logits = conv_mixer(xs, ffn, mods[i], norm_g[i, 0], sc_w_in[j], sc_conv[j], sc_w_out[j],
                                        norm_g[i, 1], router_w, router_b, n_ctx)
        rows, gate = moe_ffn(h2.reshape(bsz * length, dm), logits.reshape(bsz * length, LANES),
                             i, moe_w_gu, moe_b_gu, moe_w_down, moe_b_down)
        ffn = tuple(a.reshape(bsz, length, dm) for a in rows) + (gate.reshape(bsz, length, TOP_K), mods[i])
    return final_norm(xs, ffn, final_norm_g, n_ctx)
```

```python
import functools
import math

import jax
import jax.numpy as jnp
from jax import lax
from jax.experimental import pallas as pl
from jax.experimental.pallas import tpu as pltpu

F32 = jnp.float32
BF16 = jnp.bfloat16

HEAD = 64
CHUNK = 64
GROUP = 2
GL = GROUP * HEAD
LANES = 128
GRID_W = 64
TOKEN_TILE = 256
RMS_EPS = 1e-6
GN_EPS = HEAD * 1e-5
N_EXPERTS = 32
TOP_K = 4
N_MOD = 6
SWIGLU_LIMIT = 7.0
SWIGLU_ALPHA = 1.702
EXPERT_BLOCK = 512
V7X_VMEM_LIMIT = 56 * 1024 * 1024
HI = lax.Precision.HIGHEST
SCAN_SUB = 4
SCAN_EXACT = (False, False, False)

NN = ((1,), (0,))
NT = ((1,), (1,))
TN = ((0,), (0,))


def _dot(a, b, dims=NN, exact=False):
    dn = (dims, ((), ()))
    if exact:
        return lax.dot_general(a.astype(F32), b.astype(F32), dn, precision=HI, preferred_element_type=F32)
    return lax.dot_general(a.astype(BF16), b.astype(BF16), dn, preferred_element_type=F32)


def _sigmoid(x):
    return 0.5 * jnp.tanh(0.5 * x) + 0.5


def _params(n_grid_dims):
    return pltpu.CompilerParams(dimension_semantics=("arbitrary",) * n_grid_dims,
                                vmem_limit_bytes=V7X_VMEM_LIMIT)


def _const_spec(shape):
    nd = len(shape)
    return pl.BlockSpec(tuple(shape), lambda *_: (0,) * nd, pipeline_mode=pl.Buffered(1))


def _ada_kernel(c_ref, w_ref, b_ref, o_ref):
    cnd = c_ref[...]
    o_ref[...] = _dot(cnd * _sigmoid(cnd), w_ref[...], NN, True) + b_ref[...]


def ada_mods(cond, ada_w, ada_b):
    depth, dm, _ = ada_w.shape
    rows = cond.shape[0]
    out = pl.pallas_call(
        _ada_kernel,
        grid=(depth, N_MOD),
        in_specs=[pl.BlockSpec((rows, dm), lambda i, n: (0, 0)),
                  pl.BlockSpec((None, dm, dm), lambda i, n: (i, 0, n)),
                  pl.BlockSpec((None, 1, dm), lambda i, n: (i, 0, n))],
        out_specs=pl.BlockSpec((None, rows, dm), lambda i, n: (i, 0, n)),
        out_shape=jax.ShapeDtypeStruct((depth, rows, N_MOD * dm), F32),
        compiler_params=_params(2),
        name="ada_mods",
    )(cond, ada_w, ada_b.reshape(depth, 1, N_MOD * dm))
    return out.reshape(depth, rows, N_MOD, dm)


def _modulate(x, g, shift, scale):
    ms = jnp.mean(x * x, axis=-1, keepdims=True)
    return x * lax.rsqrt(ms + RMS_EPS) * g * (1.0 + scale) + shift


def _row_neighbours(h, row_len):
    tm = h.shape[0]
    pos = lax.broadcasted_iota(jnp.int32, (tm, 1), 0) & (row_len - 1)
    prev = jnp.where(pos != 0, pltpu.roll(h, 1, axis=0), 0.0)
    nxt = jnp.where(pos != row_len - 1, pltpu.roll(h, tm - 1, axis=0), 0.0)
    return prev, nxt


def _tile_row_len(n_ctx):
    return jnp.where(pl.program_id(1) == 0, n_ctx, GRID_W)


def _head_sum(z, hsel, hsel_t):
    zh = z.astype(BF16)
    zl = (z - zh.astype(F32)).astype(BF16)
    s = _dot(zh, hsel) + _dot(zl, hsel)
    sh = s.astype(BF16)
    sl = (s - sh.astype(F32)).astype(BF16)
    return _dot(sh, hsel_t) + _dot(sl, hsel_t)


def _moe_prologue(xn, mod_ref, ng2_ref, rw_ref, rb_ref, h2_ref, lg_ref):
    h2 = _modulate(xn, ng2_ref[...], mod_ref[3:4, :], mod_ref[4:5, :])
    h2_ref[...] = h2
    lg_ref[...] = _dot(h2, rw_ref[...]) + rb_ref[...]


def _tile_specs(dm):
    tile = pl.BlockSpec((None, TOKEN_TILE, dm), lambda b, t: (b, t, 0))
    ctx_row = lambda b, t: (jnp.where(t == 0, 4, b), 0, 0)
    mod = pl.BlockSpec((None, N_MOD, dm), ctx_row)
    return tile, mod


N_FFN_REFS = TOP_K + 2


def _ffn_specs(dm, skip=0):
    rows = pl.BlockSpec((None, TOKEN_TILE, dm), lambda b, t: (b, t + skip, 0))
    gate = pl.BlockSpec((None, TOKEN_TILE, TOP_K), lambda b, t: (b, t + skip, 0))
    mod = pl.BlockSpec((None, N_MOD, dm), lambda b, t: (jnp.where(t + skip == 0, 4, b), 0, 0))
    return [rows] * TOP_K + [gate, mod]


def _add_ffn(x, ffn_refs):
    *row_refs, gate_ref, mod_ref = ffn_refs
    f = row_refs[0][...] * gate_ref[:, 0:1]
    for k in range(1, TOP_K):
        f = f + row_refs[k][...] * gate_ref[:, k:k + 1]
    return x + mod_ref[5:6, :] * f


def _rwkv_front_kernel(x_ref, *refs, n_ctx, n_ffn):
    ffn_refs, refs = refs[:n_ffn], refs[n_ffn:]
    (mod_ref, ng_ref, mu_ref, wrkv_ref, w1_ref, w2_ref, w0_ref, a1_ref, a2_ref, a0_ref, g1_ref, g2_ref,
     kkw_ref, kaw_ref, rk_ref, hsel_ref, hselt_ref,
     logw_ref, kd_ref, bv_ref, r_ref, v_ref, kk_ref, g_ref, bonus_ref, *xo_ref) = refs
    x = x_ref[...]
    if n_ffn:
        x = _add_ffn(x, ffn_refs)
        xo_ref[0][...] = x
    h = _modulate(x, ng_ref[...], mod_ref[0:1, :], mod_ref[1:2, :])
    prev, nxt = _row_neighbours(h, _tile_row_len(n_ctx))
    hb = h.astype(BF16)
    d_prev = (prev - h).astype(BF16)
    d_next = (nxt - h).astype(BF16)

    def lerp(j):
        return hb + mu_ref[2 * j:2 * j + 1, :] * d_prev + mu_ref[2 * j + 1:2 * j + 2, :] * d_next

    r = _dot(lerp(0), wrkv_ref[0])
    k = _dot(lerp(1), wrkv_ref[1])
    v = _dot(lerp(2), wrkv_ref[2])
    hw = jnp.tanh(_dot(lerp(3), w1_ref[...]))
    ha = _dot(lerp(4), a1_ref[...])
    g_ref[...] = _dot(_sigmoid(_dot(lerp(5), g1_ref[...])), g2_ref[...])

    hsel, hselt = hsel_ref[...], hselt_ref[...]
    kkraw = k * kkw_ref[...]
    kk = kkraw / jnp.maximum(jnp.sqrt(_head_sum(kkraw * kkraw, hsel, hselt)), 1e-12)
    ksum = jnp.zeros_like(k)
    for e in range(2):
        w_log = w0_ref[e:e + 1, :] + _dot(hw, w2_ref[e])
        logw_ref[e] = -math.exp(-0.5) * _sigmoid(w_log)
        a = _sigmoid(a0_ref[e:e + 1, :] + _dot(ha, a2_ref[e]))
        kd = k * (1.0 + (a - 1.0) * kaw_ref[...])
        kd_ref[e] = kd
        bv_ref[e] = kk * a
        ksum = ksum + kd
    r_ref[...] = r
    v_ref[...] = v
    kk_ref[...] = kk
    bonus_ref[...] = _head_sum(r * ksum * rk_ref[...], hsel, hselt) * v


def rwkv_front(xs, ffn, mods, norm_g, mu, w_rkv, w0, w1, w2, a0, a1, a2, g1, g2, k_k, k_a, r_k, hsel, hselt,
               n_ctx):
    bsz, length, dm = xs.shape
    lora_w, lora_a = w1.shape[-1], a1.shape[-1]
    tile, mod = _tile_specs(dm)
    dir_tile = pl.BlockSpec((2, None, TOKEN_TILE, dm), lambda b, t: (0, b, t, 0))

    def pad_rows(w, e):
        z = jnp.zeros_like(w[e])
        return jnp.concatenate([w[0] if e == 0 else z, w[1] if e == 1 else z], axis=0)

    w1c = jnp.concatenate([w1[0], w1[1]], axis=1).astype(BF16)
    a1c = jnp.concatenate([a1[0], a1[1]], axis=1).astype(BF16)
    w2p = jnp.stack([pad_rows(w2, 0), pad_rows(w2, 1)]).astype(BF16)
    a2p = jnp.stack([pad_rows(a2, 0), pad_rows(a2, 1)]).astype(BF16)
    vec = lambda a: a.reshape(1, dm)
    consts = [vec(norm_g), mu.reshape(12, dm).astype(BF16), w_rkv.astype(BF16), w1c, w2p, w0, a1c, a2p, a0,
              g1.astype(BF16), g2.astype(BF16), vec(k_k), vec(k_a), vec(r_k), hsel, hselt]
    sds = jax.ShapeDtypeStruct
    return pl.pallas_call(
        functools.partial(_rwkv_front_kernel, n_ctx=n_ctx, n_ffn=len(ffn)),
        grid=(bsz, length // TOKEN_TILE),
        in_specs=[tile] + (_ffn_specs(dm) if ffn else []) + [mod] + [_const_spec(a.shape) for a in consts],
        out_specs=[dir_tile, dir_tile, dir_tile] + [tile] * (6 if ffn else 5),
        out_shape=[sds((2, bsz, length, dm), F32)] * 3 + [sds((bsz, length, dm), F32)] * (6 if ffn else 5),
        compiler_params=_params(2),
        name="rwkv_front",
    )(xs, *ffn, mods, *consts)


def _wkv_kernel(*refs, exact):
    s_ref = refs[-1]

    @pl.when(pl.program_id(1) == 0)
    def _():
        s_ref[...] = jnp.zeros_like(s_ref)

    for sub in range(SCAN_SUB):
        _wkv_chunk_pair(sub, *refs, exact=exact)


def _wkv_chunk_pair(sub, lwf_ref, kdf_ref, bf_ref, rf_ref, vf_ref, kkf_ref, lwb_ref, kdb_ref, bb_ref, rb_ref,
                    vb_ref, kkb_ref, yf_ref, yb_ref, s_ref, *, exact):
    ex_a, ex_t, ex_s = exact
    n_groups = s_ref.shape[1]
    row = lax.broadcasted_iota(jnp.int32, (CHUNK, CHUNK), 0)
    col = lax.broadcasted_iota(jnp.int32, (CHUNK, CHUNK), 1)
    rowc = lax.broadcasted_iota(jnp.int32, (CHUNK, GL), 0)
    colc = lax.broadcasted_iota(jnp.int32, (CHUNK, GL), 1) % HEAD
    brow = lax.broadcasted_iota(jnp.int32, (GL, GL), 0) // HEAD
    bcol = lax.broadcasted_iota(jnp.int32, (GL, GL), 1) // HEAD
    bmask = brow == bcol
    eye = rowc == colc

    def bd(y):
        return jnp.where(bmask, jnp.concatenate([y] * GROUP, axis=0), 0.0)

    def bdmm(x, y):
        return _dot(x, bd(y), NN, ex_t)

    st = []
    for d, (lw_ref, kd_ref, b_ref, r_ref, v_ref, kk_ref) in enumerate(
            ((lwf_ref, kdf_ref, bf_ref, rf_ref, vf_ref, kkf_ref),
             (lwb_ref, kdb_ref, bb_ref, rb_ref, vb_ref, kkb_ref))):
        sgn = 1 - 2 * d
        tri = ((row - col) * sgn >= 0).astype(F32)
        dcat = (rowc - colc) * sgn
        strict, incl = dcat > 0, dcat >= 0
        first = (sub if d == 0 else SCAN_SUB - 1 - sub) * CHUNK
        rows = slice(first, first + CHUNK)
        lw = lw_ref[rows, :]
        lw1 = lw.astype(BF16)
        rem = lw - lw1.astype(F32)
        lw2 = rem.astype(BF16)
        lw3 = (rem - lw2.astype(F32)).astype(BF16)
        cum = _dot(jnp.concatenate([tri] * 3, axis=1), jnp.concatenate([lw1, lw2, lw3], axis=0))
        e_tot = jnp.exp(jnp.sum(lw, axis=0, keepdims=True))
        e_neg = jnp.exp(-cum)
        kkp = kk_ref[rows, :] * jnp.exp(cum - lw)
        rp = r_ref[rows, :] * jnp.exp(cum)
        kinv = kd_ref[rows, :] * e_neg
        binv = b_ref[rows, :] * e_neg
        kdec = kinv * e_tot
        bdec = binv * e_tot
        v = v_ref[rows, :]
        for g in range(n_groups):
            sl = slice(g * GL, (g + 1) * GL)
            st.append(dict(d=d, g=g, sl=sl, rows=rows, strict=strict, incl=incl, v=v[:, sl], kinv=kinv[:, sl],
                           binv=binv[:, sl], e_tot=e_tot[:, sl],
                           lhs2=jnp.concatenate([kkp[:, sl], rp[:, sl]], axis=0),
                           dec=jnp.concatenate([kdec[:, sl], -bdec[:, sl]], axis=0)))
    n = range(len(st))
    akb = [_dot(st[i]["lhs2"], jnp.concatenate([bd(st[i]["kinv"]), bd(st[i]["binv"])], axis=0), NT, ex_a)
           for i in n]
    ak = [akb[i][:, :GL] for i in n]
    ab = [akb[i][:, GL:] for i in n]
    a_kr = [jnp.concatenate([jnp.where(st[i]["strict"], ak[i][:CHUNK], 0.0),
                             jnp.where(st[i]["incl"], ak[i][CHUNK:], 0.0)], axis=0) for i in n]
    arb = [jnp.where(st[i]["incl"], ab[i][CHUNK:], 0.0) for i in n]
    mp = [-jnp.where(st[i]["strict"], ab[i][:CHUNK], 0.0) for i in n]
    t = [jnp.where(eye, 1.0, 0.0) + mp[i] for i in n]
    mp = [bdmm(mp[i], mp[i]) for i in n]
    m = 2
    while 2 * m < CHUNK:
        r2 = [bdmm(jnp.concatenate([mp[i], t[i]], axis=0), mp[i]) for i in n]
        mp = [r2[i][:CHUNK] for i in n]
        t = [t[i] + r2[i][CHUNK:] for i in n]
        m *= 2
    t = [t[i] + bdmm(t[i], mp[i]) for i in n]
    av = [bdmm(a_kr[i], st[i]["v"]) for i in n]
    s = [s_ref[st[i]["d"], st[i]["g"]] for i in n]
    zs = [_dot(st[i]["lhs2"], s[i], NT, ex_s) for i in n]
    u = [bdmm(t[i], zs[i][:CHUNK] + av[i][:CHUNK]) for i in n]
    for i in n:
        y_ref = yb_ref if st[i]["d"] else yf_ref
        y_ref[st[i]["rows"], st[i]["sl"]] = zs[i][CHUNK:] + av[i][CHUNK:] - bdmm(arb[i], u[i])
    upd = [_dot(jnp.concatenate([st[i]["v"], u[i]], axis=0), st[i]["dec"], TN, ex_s) for i in n]
    for i in n:
        s_ref[st[i]["d"], st[i]["g"]] = s[i] * st[i]["e_tot"] + jnp.where(bmask, upd[i], 0.0)


def wkv_scan(logw, kdir, bvec, r, v, kk, n_ctx, exact=SCAN_EXACT):
    _, bsz, length, dm = logw.shape
    blk = SCAN_SUB * CHUNK
    assert n_ctx % blk == 0 and length % blk == 0
    nc = length // blk
    ncc = n_ctx // blk

    def back(c):
        return jnp.where(c < ncc, ncc - 1 - c, nc - 1 + ncc - c)

    dir_f = pl.BlockSpec((None, None, blk, dm), lambda b, c: (0, b, c, 0))
    dir_b = pl.BlockSpec((None, None, blk, dm), lambda b, c: (1, b, back(c), 0))
    sh_f = pl.BlockSpec((None, blk, dm), lambda b, c: (b, c, 0))
    sh_b = pl.BlockSpec((None, blk, dm), lambda b, c: (b, back(c), 0))
    sds = jax.ShapeDtypeStruct((bsz, length, dm), F32)
    return pl.pallas_call(
        functools.partial(_wkv_kernel, exact=exact),
        grid=(bsz, nc),
        in_specs=[dir_f, dir_f, dir_f, sh_f, sh_f, sh_f, dir_b, dir_b, dir_b, sh_b, sh_b, sh_b],
        out_specs=[sh_f, sh_b],
        out_shape=[sds, sds],
        scratch_shapes=[pltpu.VMEM((2, dm // GL, GL, GL), F32)],
        compiler_params=_params(2),
        name="wkv_scan",
    )(logw, kdir, bvec, r, v, kk, logw, kdir, bvec, r, v, kk)


def _rwkv_readout_kernel(x_ref, mod_ref, yf_ref, yb_ref, bonus_ref, g_ref, lng_ref, lnb_ref, wo_ref, hsel_ref,
                         hselt_ref, ng2_ref, rw_ref, rb_ref, xo_ref, h2_ref, lg_ref):
    hsel, hselt = hsel_ref[...], hselt_ref[...]
    y = yf_ref[...] + yb_ref[...]
    yc = y - _head_sum(y, hsel, hselt) * (1.0 / HEAD)
    var = _head_sum(yc * yc, hsel, hselt) * (1.0 / HEAD)
    yn = yc * lax.rsqrt(var + GN_EPS) * lng_ref[...] + lnb_ref[...]
    z = (yn + bonus_ref[...]) * g_ref[...]
    xn = x_ref[...] + mod_ref[2:3, :] * _dot(z, wo_ref[...])
    xo_ref[...] = xn
    _moe_prologue(xn, mod_ref, ng2_ref, rw_ref, rb_ref, h2_ref, lg_ref)


def rwkv_readout(xs, mods, y_fwd, y_bwd, bonus, g, ln_g, ln_b, w_o, hsel, hselt, ng2, router_w, router_b):
    bsz, length, dm = xs.shape
    tile, mod = _tile_specs(dm)
    lg_tile = pl.BlockSpec((None, TOKEN_TILE, LANES), lambda b, t: (b, t, 0))
    vec = lambda a: a.reshape(1, -1)
    consts = [vec(ln_g), vec(ln_b), w_o.astype(BF16), hsel, hselt, vec(ng2), router_w, router_b]
    sds = jax.ShapeDtypeStruct
    return pl.pallas_call(
        _rwkv_readout_kernel,
        grid=(bsz, length // TOKEN_TILE),
        in_specs=[tile, mod, tile, tile, tile, tile] + [_const_spec(a.shape) for a in consts],
        out_specs=[tile, tile, lg_tile],
        out_shape=[sds((bsz, length, dm), F32), sds((bsz, length, dm), F32),
                   sds((bsz, length, LANES), F32)],
        compiler_params=_params(2),
        name="rwkv_readout",
    )(xs, mods, y_fwd, y_bwd, bonus, g, *consts)


def _conv_kernel(x_ref, *refs, n_ctx, n_ffn):
    ffn_refs, refs = refs[:n_ffn], refs[n_ffn:]
    mod_ref, ng_ref, win_ref, cw_ref, wout_ref, ng2_ref, rw_ref, rb_ref, xo_ref, h2_ref, lg_ref = refs
    x = x_ref[...]
    if n_ffn:
        x = _add_ffn(x, ffn_refs)
    dm = x.shape[1]
    h = _modulate(x, ng_ref[...], mod_ref[0:1, :], mod_ref[1:2, :])
    hw = _dot(h, win_ref[...])
    b_gate, c_gate, u = hw[:, :dm], hw[:, dm:2 * dm], hw[:, 2 * dm:]
    z = c_gate * u
    z_prev, z_next = _row_neighbours(z, _tile_row_len(n_ctx))
    z = cw_ref[0:1, :] * z_prev + cw_ref[1:2, :] * z + cw_ref[2:3, :] * z_next
    xn = x + mod_ref[2:3, :] * _dot(b_gate * z, wout_ref[...])
    xo_ref[...] = xn
    _moe_prologue(xn, mod_ref, ng2_ref, rw_ref, rb_ref, h2_ref, lg_ref)


def conv_mixer(xs, ffn, mods, ng1, w_in, conv_w, w_out, ng2, router_w, router_b, n_ctx):
    bsz, length, dm = xs.shape
    tile, mod = _tile_specs(dm)
    lg_tile = pl.BlockSpec((None, TOKEN_TILE, LANES), lambda b, t: (b, t, 0))
    vec = lambda a: a.reshape(1, -1)
    consts = [vec(ng1), w_in.astype(BF16), conv_w, w_out.astype(BF16), vec(ng2), router_w, router_b]
    sds = jax.ShapeDtypeStruct
    return pl.pallas_call(
        functools.partial(_conv_kernel, n_ctx=n_ctx, n_ffn=len(ffn)),
        grid=(bsz, length // TOKEN_TILE),
        in_specs=[tile] + (_ffn_specs(dm) if ffn else []) + [mod] + [_const_spec(a.shape) for a in consts],
        out_specs=[tile, tile, lg_tile],
        out_shape=[sds((bsz, length, dm), F32), sds((bsz, length, dm), F32),
                   sds((bsz, length, LANES), F32)],
        compiler_params=_params(2),
        name="conv_mixer",
    )(xs, *ffn, mods, *consts)


def _expert_kernel(be_ref, nu_ref, xb_ref, wgu_ref, bgu_ref, wd_ref, bdn_ref, o_ref, wgu_bf, wd_bf):
    i = pl.program_id(0)

    @pl.when(i < nu_ref[0])
    def _():
        new_expert = jnp.logical_or(i == 0, be_ref[i] != be_ref[jnp.maximum(i - 1, 0)])

        @pl.when(new_expert)
        def _():
            wgu_bf[...] = wgu_ref[...].astype(BF16)
            wd_bf[...] = wd_ref[...].astype(BF16)

        hgu = _dot(xb_ref[...], wgu_bf[...]) + bgu_ref[...]
        ff = hgu.shape[1] // 2
        g_ = jnp.minimum(hgu[:, :ff], SWIGLU_LIMIT)
        u_ = jnp.clip(hgu[:, ff:], -SWIGLU_LIMIT, SWIGLU_LIMIT)
        act = (u_ + 1.0) * (g_ * _sigmoid(SWIGLU_ALPHA * g_))
        o_ref[...] = _dot(act, wd_bf[...]) + bdn_ref[...]


def expert_ffn(block_e, n_used, xb, layer, w_gu, b_gu, w_down, b_down):
    n_rows, dm = xb.shape
    depth, n_e, _, ff2 = w_gu.shape
    grid_spec = pltpu.PrefetchScalarGridSpec(
        num_scalar_prefetch=2,
        grid=(n_rows // EXPERT_BLOCK,),
        in_specs=[pl.BlockSpec((EXPERT_BLOCK, dm), lambda i, be, nu: (i, 0)),
                  pl.BlockSpec((None, None, dm, ff2), lambda i, be, nu: (layer, be[i], 0, 0)),
                  pl.BlockSpec((None, None, 1, ff2), lambda i, be, nu: (layer, be[i], 0, 0)),
                  pl.BlockSpec((None, None, ff2 // 2, dm), lambda i, be, nu: (layer, be[i], 0, 0)),
                  pl.BlockSpec((None, None, 1, dm), lambda i, be, nu: (layer, be[i], 0, 0))],
        out_specs=pl.BlockSpec((EXPERT_BLOCK, dm), lambda i, be, nu: (i, 0)),
        scratch_shapes=[pltpu.VMEM((dm, ff2), BF16), pltpu.VMEM((ff2 // 2, dm), BF16)])
    return pl.pallas_call(
        _expert_kernel,
        grid_spec=grid_spec,
        out_shape=jax.ShapeDtypeStruct((n_rows, dm), F32),
        compiler_params=_params(1),
        name="expert_ffn",
    )(block_e, n_used, xb, w_gu, b_gu.reshape(depth, n_e, 1, ff2), w_down, b_down.reshape(depth, n_e, 1, dm))


def moe_ffn(h2, logits, layer, w_gu, b_gu, w_down, b_down):
    t, dm = h2.shape
    top_logit, top_e = lax.top_k(logits[:, :N_EXPERTS], TOP_K)
    gate = jax.nn.softmax(top_logit, axis=-1)
    expert = top_e.reshape(-1).astype(jnp.int32)
    n_assign = t * TOP_K
    eids = jnp.arange(N_EXPERTS, dtype=jnp.int32)
    onehot = (expert[:, None] == eids[None, :]).astype(jnp.int32)
    csum = jnp.cumsum(onehot, axis=0)
    counts = csum[-1]
    padded = (counts + EXPERT_BLOCK - 1) // EXPERT_BLOCK * EXPERT_BLOCK
    pad_end = jnp.cumsum(padded)
    pad_start = pad_end - padded
    dest = jnp.sum(onehot * (csum - 1 + pad_start[None, :]), axis=1)
    n_blocks = -(-(n_assign + N_EXPERTS * (EXPERT_BLOCK - 1)) // EXPERT_BLOCK)
    n_rows = n_blocks * EXPERT_BLOCK
    block_start = jnp.arange(n_blocks, dtype=jnp.int32) * EXPERT_BLOCK
    block_e = jnp.minimum(jnp.sum((pad_end[None, :] <= block_start[:, None]).astype(jnp.int32), axis=1),
                          N_EXPERTS - 1)
    token = jnp.arange(n_assign, dtype=jnp.int32) // TOP_K
    row_tok = jnp.zeros((n_rows,), jnp.int32).at[dest].add(token, unique_indices=True)
    n_used = (pad_end[-1:] // EXPERT_BLOCK).astype(jnp.int32)
    yb = expert_ffn(block_e, n_used, h2[row_tok], layer, w_gu, b_gu, w_down, b_down)
    dest_km = dest.reshape(t, TOP_K).T
    return [yb[dest_km[k]] for k in range(TOP_K)], gate


def _final_norm_kernel(x_ref, *refs):
    *ffn_refs, g_ref, o_ref = refs
    x = _add_ffn(x_ref[...], ffn_refs)
    ms = jnp.mean(x * x, axis=-1, keepdims=True)
    o_ref[...] = x * lax.rsqrt(ms + RMS_EPS) * g_ref[...]


def final_norm(xs, ffn, g, n_ctx):
    bsz, length, dm = xs.shape
    skip = n_ctx // TOKEN_TILE
    return pl.pallas_call(
        _final_norm_kernel,
        grid=(bsz, length // TOKEN_TILE - skip),
        in_specs=[pl.BlockSpec((None, TOKEN_TILE, dm), lambda b, t: (b, t + skip, 0))]
        + _ffn_specs(dm, skip) + [pl.BlockSpec((1, dm), lambda b, t: (0, 0))],
        out_specs=pl.BlockSpec((None, TOKEN_TILE, dm), lambda b, t: (b, t, 0)),
        out_shape=jax.ShapeDtypeStruct((bsz, length - n_ctx, dm), F32),
        compiler_params=_params(2),
        name="final_norm",
    )(xs, *ffn, g.reshape(1, dm))


def kernel(x, c, ctx, c_ctx, ada_w, ada_b, norm_g, final_norm_g, rw_mu, rw_w_rkv, rw_w0, rw_w1, rw_w2, rw_a0, rw_a1, rw_a2, rw_g1, rw_g2, rw_k_k, rw_k_a, rw_r_k, rw_ln_g, rw_ln_b, rw_w_o, sc_w_in, sc_conv, sc_w_out, moe_router_w, moe_router_b, moe_w_gu, moe_b_gu, moe_w_down, moe_b_down):
    bsz, seq, dm = x.shape
    n_ctx = ctx.shape[1]
    depth = ada_w.shape[0]
    assert n_ctx == TOKEN_TILE and seq % TOKEN_TILE == 0 and dm % GL == 0 and bsz == 4
    length = n_ctx + seq

    cond = jnp.concatenate([c, c_ctx[None], jnp.zeros((3, dm), F32)], axis=0)
    mods = ada_mods(cond, ada_w, ada_b)
    xs = jnp.concatenate([ctx, x], axis=1)

    head_of_lane = jnp.arange(dm, dtype=jnp.int32) // HEAD
    hsel = (head_of_lane[:, None] == jnp.arange(LANES, dtype=jnp.int32)[None, :]).astype(BF16)
    hselt = hsel.T
    ffn = ()
    for i in range(depth):
        j = i // 2
        router_w = jnp.pad(moe_router_w[i], ((0, 0), (0, LANES - N_EXPERTS))).astype(BF16)
        router_b = jnp.pad(moe_router_b[i], (0, LANES - N_EXPERTS)).reshape(1, LANES)
        if i % 2 == 0:
            logw, kdir, bvec, r, v, kk, g, bonus, *x_new = rwkv_front(
                xs, ffn, mods[i], norm_g[i, 0], rw_mu[j], rw_w_rkv[j], rw_w0[j], rw_w1[j], rw_w2[j], rw_a0[j],
                rw_a1[j], rw_a2[j], rw_g1[j], rw_g2[j], rw_k_k[j], rw_k_a[j], rw_r_k[j], hsel, hselt, n_ctx)
            xs = x_new[0] if ffn else xs
            y_fwd, y_bwd = wkv_scan(logw, kdir, bvec, r, v, kk, n_ctx)
            xs, h2, logits = rwkv_readout(xs, mods[i], y_fwd, y_bwd, bonus, g, rw_ln_g[j], rw_ln_b[j], rw_w_o[j],
                                          hsel, hselt, norm_g[i, 1], router_w, router_b)
        else:
            xs, h2, logits = conv_mixer(xs, ffn, mods[i], norm_g[i, 0], sc_w_in[j], sc_conv[j], sc_w_out[j],
                                        norm_g[i, 1], router_w, router_b, n_ctx)
        rows, gate = moe_ffn(h2.reshape(bsz * length, dm), logits.reshape(bsz * length, LANES),
                             i, moe_w_gu, moe_b_gu, moe_w_down, moe_b_down)
        ffn = tuple(a.reshape(bsz, length, dm) for a in rows) + (gate.reshape(bsz, length, TOP_K), mods[i])
    return final_norm(xs, ffn, final_norm_g, n_ctx)
```

```python
import functools
import math

import jax
import jax.numpy as jnp
from jax import lax
from jax.experimental import pallas as pl
from jax.experimental.pallas import tpu as pltpu

F32 = jnp.float32
BF16 = jnp.bfloat16

HEAD = 64
CHUNK = 64
GROUP = 2
GL = GROUP * HEAD
LANES = 128
GRID_W = 64
TOKEN_TILE = 256
RMS_EPS = 1e-6
GN_EPS = HEAD * 1e-5
N_EXPERTS = 32
TOP_K = 4
N_MOD = 6
SWIGLU_LIMIT = 7.0
SWIGLU_ALPHA = 1.702
EXPERT_BLOCK = 512
V7X_VMEM_LIMIT = 56 * 1024 * 1024
HI = lax.Precision.HIGHEST
SCAN_SUB = 4

NN = ((1,), (0,))
NT = ((1,), (1,))
TN = ((0,), (0,))


def _dot(a, b, dims=NN, exact=False):
    dn = (dims, ((), ()))
    if exact:
        return lax.dot_general(a.astype(F32), b.astype(F32), dn, precision=HI, preferred_element_type=F32)
    return lax.dot_general(a.astype(BF16), b.astype(BF16), dn, preferred_element_type=F32)


def _sigmoid(x):
    return 0.5 * jnp.tanh(0.5 * x) + 0.5


def _params(n_grid_dims):
    return pltpu.CompilerParams(dimension_semantics=("arbitrary",) * n_grid_dims,
                                vmem_limit_bytes=V7X_VMEM_LIMIT)


def _const_spec(shape):
    nd = len(shape)
    return pl.BlockSpec(tuple(shape), lambda *_: (0,) * nd, pipeline_mode=pl.Buffered(1))


def _ada_kernel(c_ref, w_ref, b_ref, o_ref):
    cnd = c_ref[...]
    o_ref[...] = _dot(cnd * _sigmoid(cnd), w_ref[...], NN, True) + b_ref[...]


def ada_mods(cond, ada_w, ada_b):
    depth, dm, _ = ada_w.shape
    rows = cond.shape[0]
    out = pl.pallas_call(
        _ada_kernel,
        grid=(depth, N_MOD),
        in_specs=[pl.BlockSpec((rows, dm), lambda i, n: (0, 0)),
                  pl.BlockSpec((None, dm, dm), lambda i, n: (i, 0, n)),
                  pl.BlockSpec((None, 1, dm), lambda i, n: (i, 0, n))],
        out_specs=pl.BlockSpec((None, rows, dm), lambda i, n: (i, 0, n)),
        out_shape=jax.ShapeDtypeStruct((depth, rows, N_MOD * dm), F32),
        compiler_params=_params(2),
        name="ada_mods",
    )(cond, ada_w, ada_b.reshape(depth, 1, N_MOD * dm))
    return out.reshape(depth, rows, N_MOD, dm)


def _modulate(x, g, shift, scale):
    ms = jnp.mean(x * x, axis=-1, keepdims=True)
    return x * lax.rsqrt(ms + RMS_EPS) * g * (1.0 + scale) + shift


def _row_neighbours(h, row_len):
    tm = h.shape[0]
    pos = lax.broadcasted_iota(jnp.int32, (tm, 1), 0) & (row_len - 1)
    prev = jnp.where(pos != 0, pltpu.roll(h, 1, axis=0), 0.0)
    nxt = jnp.where(pos != row_len - 1, pltpu.roll(h, tm - 1, axis=0), 0.0)
    return prev, nxt


def _tile_row_len(n_ctx):
    return jnp.where(pl.program_id(1) == 0, n_ctx, GRID_W)


def _head_sum(z, hsel, hsel_t):
    zh = z.astype(BF16)
    zl = (z - zh.astype(F32)).astype(BF16)
    s = _dot(zh, hsel) + _dot(zl, hsel)
    sh = s.astype(BF16)
    sl = (s - sh.astype(F32)).astype(BF16)
    return _dot(sh, hsel_t) + _dot(sl, hsel_t)


def _moe_prologue(xn, mod_ref, ng2_ref, rw_ref, rb_ref, h2_ref, lg_ref):
    h2 = _modulate(xn, ng2_ref[...], mod_ref[3:4, :], mod_ref[4:5, :])
    h2_ref[...] = h2
    lg_ref[...] = _dot(h2, rw_ref[...]) + rb_ref[...]


def _tile_specs(dm, bsz):
    tile = pl.BlockSpec((None, TOKEN_TILE, dm), lambda b, t: (b, t, 0))
    mod = pl.BlockSpec((None, N_MOD, dm), lambda b, t: (jnp.where(t == 0, bsz, b), 0, 0))
    return tile, mod


N_FFN_REFS = TOP_K + 2


def _ffn_specs(dm, bsz, skip=0):
    rows = pl.BlockSpec((None, TOKEN_TILE, dm), lambda b, t: (b, t + skip, 0))
    gate = pl.BlockSpec((None, TOKEN_TILE, TOP_K), lambda b, t: (b, t + skip, 0))
    mod = pl.BlockSpec((None, N_MOD, dm), lambda b, t: (jnp.where(t + skip == 0, bsz, b), 0, 0))
    return [rows] * TOP_K + [gate, mod]


def _add_ffn(x, ffn_refs):
    *row_refs, gate_ref, mod_ref = ffn_refs
    f = row_refs[0][...] * gate_ref[:, 0:1]
    for k in range(1, TOP_K):
        f = f + row_refs[k][...] * gate_ref[:, k:k + 1]
    return x + mod_ref[5:6, :] * f


def _rwkv_front_kernel(x_ref, *refs, n_ctx, n_ffn):
    ffn_refs, refs = refs[:n_ffn], refs[n_ffn:]
    (mod_ref, ng_ref, mu_ref, wrkv_ref, w1_ref, w2_ref, w0_ref, a1_ref, a2_ref, a0_ref, g1_ref, g2_ref,
     kkw_ref, kaw_ref, rk_ref, hsel_ref, hselt_ref,
     logw_ref, kd_ref, bv_ref, r_ref, v_ref, kk_ref, g_ref, bonus_ref, *xo_ref) = refs
    x = x_ref[...]
    if n_ffn:
        x = _add_ffn(x, ffn_refs)
        xo_ref[0][...] = x
    h = _modulate(x, ng_ref[...], mod_ref[0:1, :], mod_ref[1:2, :])
    prev, nxt = _row_neighbours(h, _tile_row_len(n_ctx))
    hb = h.astype(BF16)
    d_prev = (prev - h).astype(BF16)
    d_next = (nxt - h).astype(BF16)

    def lerp(j):
        return hb + mu_ref[2 * j:2 * j + 1, :] * d_prev + mu_ref[2 * j + 1:2 * j + 2, :] * d_next

    r = _dot(lerp(0), wrkv_ref[0])
    k = _dot(lerp(1), wrkv_ref[1])
    v = _dot(lerp(2), wrkv_ref[2])
    hw = jnp.tanh(_dot(lerp(3), w1_ref[...]))
    ha = _dot(lerp(4), a1_ref[...])
    g_ref[...] = _dot(_sigmoid(_dot(lerp(5), g1_ref[...])), g2_ref[...])

    hsel, hselt = hsel_ref[...], hselt_ref[...]
    kkraw = k * kkw_ref[...]
    kk = kkraw / jnp.maximum(jnp.sqrt(_head_sum(kkraw * kkraw, hsel, hselt)), 1e-12)
    ksum = jnp.zeros_like(k)
    for e in range(2):
        w_log = w0_ref[e:e + 1, :] + _dot(hw, w2_ref[e])
        logw_ref[e] = -math.exp(-0.5) * _sigmoid(w_log)
        a = _sigmoid(a0_ref[e:e + 1, :] + _dot(ha, a2_ref[e]))
        kd = k * (1.0 + (a - 1.0) * kaw_ref[...])
        kd_ref[e] = kd
        bv_ref[e] = kk * a
        ksum = ksum + kd
    r_ref[...] = r
    v_ref[...] = v
    kk_ref[...] = kk
    bonus_ref[...] = _head_sum(r * ksum * rk_ref[...], hsel, hselt) * v


def rwkv_front(xs, ffn, mods, norm_g, mu, w_rkv, w0, w1, w2, a0, a1, a2, g1, g2, k_k, k_a, r_k, hsel, hselt,
               n_ctx):
    bsz, length, dm = xs.shape
    lora_w, lora_a = w1.shape[-1], a1.shape[-1]
    tile, mod = _tile_specs(dm, bsz)
    dir_tile = pl.BlockSpec((2, None, TOKEN_TILE, dm), lambda b, t: (0, b, t, 0))

    def pad_rows(w, e):
        z = jnp.zeros_like(w[e])
        return jnp.concatenate([w[0] if e == 0 else z, w[1] if e == 1 else z], axis=0)

    w1c = jnp.concatenate([w1[0], w1[1]], axis=1).astype(BF16)
    a1c = jnp.concatenate([a1[0], a1[1]], axis=1).astype(BF16)
    w2p = jnp.stack([pad_rows(w2, 0), pad_rows(w2, 1)]).astype(BF16)
    a2p = jnp.stack([pad_rows(a2, 0), pad_rows(a2, 1)]).astype(BF16)
    vec = lambda a: a.reshape(1, dm)
    consts = [vec(norm_g), mu.reshape(12, dm).astype(BF16), w_rkv.astype(BF16), w1c, w2p, w0, a1c, a2p, a0,
              g1.astype(BF16), g2.astype(BF16), vec(k_k), vec(k_a), vec(r_k), hsel, hselt]
    sds = jax.ShapeDtypeStruct
    return pl.pallas_call(
        functools.partial(_rwkv_front_kernel, n_ctx=n_ctx, n_ffn=len(ffn)),
        grid=(bsz, length // TOKEN_TILE),
        in_specs=[tile] + (_ffn_specs(dm, bsz) if ffn else []) + [mod] + [_const_spec(a.shape) for a in consts],
        out_specs=[dir_tile, dir_tile, dir_tile] + [tile] * (6 if ffn else 5),
        out_shape=[sds((2, bsz, length, dm), F32)] * 3 + [sds((bsz, length, dm), F32)] * (6 if ffn else 5),
        compiler_params=_params(2),
        name="rwkv_front",
    )(xs, *ffn, mods, *consts)


def _wkv_kernel(*refs):
    s_ref = refs[-1]

    @pl.when(pl.program_id(1) == 0)
    def _():
        s_ref[...] = jnp.zeros_like(s_ref)

    for sub in range(SCAN_SUB):
        _wkv_chunk_pair(sub, *refs)


def _wkv_chunk_pair(sub, lwf_ref, kdf_ref, bf_ref, rf_ref, vf_ref, kkf_ref, lwb_ref, kdb_ref, bb_ref, rb_ref,
                    vb_ref, kkb_ref, yf_ref, yb_ref, s_ref):
    n_groups = s_ref.shape[1]
    row = lax.broadcasted_iota(jnp.int32, (CHUNK, CHUNK), 0)
    col = lax.broadcasted_iota(jnp.int32, (CHUNK, CHUNK), 1)
    rowc = lax.broadcasted_iota(jnp.int32, (CHUNK, GL), 0)
    colc = lax.broadcasted_iota(jnp.int32, (CHUNK, GL), 1) % HEAD
    brow = lax.broadcasted_iota(jnp.int32, (GL, GL), 0) // HEAD
    bcol = lax.broadcasted_iota(jnp.int32, (GL, GL), 1) // HEAD
    bmask = brow == bcol
    eye = rowc == colc

    def bd(y):
        return jnp.where(bmask, jnp.concatenate([y] * GROUP, axis=0), 0.0)

    def bdmm(x, y):
        return _dot(x, bd(y))

    st = []
    for d, (lw_ref, kd_ref, b_ref, r_ref, v_ref, kk_ref) in enumerate(
            ((lwf_ref, kdf_ref, bf_ref, rf_ref, vf_ref, kkf_ref),
             (lwb_ref, kdb_ref, bb_ref, rb_ref, vb_ref, kkb_ref))):
        sgn = 1 - 2 * d
        tri = ((row - col) * sgn >= 0).astype(F32)
        dcat = (rowc - colc) * sgn
        strict, incl = dcat > 0, dcat >= 0
        first = (sub if d == 0 else SCAN_SUB - 1 - sub) * CHUNK
        rows = slice(first, first + CHUNK)
        lw = lw_ref[rows, :]
        lw1 = lw.astype(BF16)
        rem = lw - lw1.astype(F32)
        lw2 = rem.astype(BF16)
        lw3 = (rem - lw2.astype(F32)).astype(BF16)
        cum = _dot(jnp.concatenate([tri] * 3, axis=1), jnp.concatenate([lw1, lw2, lw3], axis=0))
        e_tot = jnp.exp(jnp.sum(lw, axis=0, keepdims=True))
        e_neg = jnp.exp(-cum)
        kkp = kk_ref[rows, :] * jnp.exp(cum - lw)
        rp = r_ref[rows, :] * jnp.exp(cum)
        kinv = kd_ref[rows, :] * e_neg
        binv = b_ref[rows, :] * e_neg
        kdec = kinv * e_tot
        bdec = binv * e_tot
        v = v_ref[rows, :]
        for g in range(n_groups):
            sl = slice(g * GL, (g + 1) * GL)
            st.append(dict(d=d, g=g, sl=sl, rows=rows, strict=strict, incl=incl, v=v[:, sl], kinv=kinv[:, sl],
                           binv=binv[:, sl], e_tot=e_tot[:, sl],
                           lhs2=jnp.concatenate([kkp[:, sl], rp[:, sl]], axis=0),
                           dec=jnp.concatenate([kdec[:, sl], -bdec[:, sl]], axis=0)))
    n = range(len(st))
    akb = [_dot(st[i]["lhs2"], jnp.concatenate([bd(st[i]["kinv"]), bd(st[i]["binv"])], axis=0), NT)
           for i in n]
    ak = [akb[i][:, :GL] for i in n]
    ab = [akb[i][:, GL:] for i in n]
    a_kr = [jnp.concatenate([jnp.where(st[i]["strict"], ak[i][:CHUNK], 0.0),
                             jnp.where(st[i]["incl"], ak[i][CHUNK:], 0.0)], axis=0) for i in n]
    arb = [jnp.where(st[i]["incl"], ab[i][CHUNK:], 0.0) for i in n]
    mp = [-jnp.where(st[i]["strict"], ab[i][:CHUNK], 0.0) for i in n]
    t = [jnp.where(eye, 1.0, 0.0) + mp[i] for i in n]
    mp = [bdmm(mp[i], mp[i]) for i in n]
    m = 2
    while 2 * m < CHUNK:
        r2 = [bdmm(jnp.concatenate([mp[i], t[i]], axis=0), mp[i]) for i in n]
        mp = [r2[i][:CHUNK] for i in n]
        t = [t[i] + r2[i][CHUNK:] for i in n]
        m *= 2
    t = [t[i] + bdmm(t[i], mp[i]) for i in n]
    av = [bdmm(a_kr[i], st[i]["v"]) for i in n]
    s = [s_ref[st[i]["d"], st[i]["g"]] for i in n]
    zs = [_dot(st[i]["lhs2"], s[i], NT) for i in n]
    u = [bdmm(t[i], zs[i][:CHUNK] + av[i][:CHUNK]) for i in n]
    for i in n:
        y_ref = yb_ref if st[i]["d"] else yf_ref
        y_ref[st[i]["rows"], st[i]["sl"]] = zs[i][CHUNK:] + av[i][CHUNK:] - bdmm(arb[i], u[i])
    upd = [_dot(jnp.concatenate([st[i]["v"], u[i]], axis=0), st[i]["dec"], TN) for i in n]
    for i in n:
        s_ref[st[i]["d"], st[i]["g"]] = s[i] * st[i]["e_tot"] + jnp.where(bmask, upd[i], 0.0)


def wkv_scan(logw, kdir, bvec, r, v, kk, n_ctx):
    _, bsz, length, dm = logw.shape
    blk = SCAN_SUB * CHUNK
    assert n_ctx % blk == 0 and length % blk == 0
    nc = length // blk
    ncc = n_ctx // blk

    def back(c):
        return jnp.where(c < ncc, ncc - 1 - c, nc - 1 + ncc - c)

    dir_f = pl.BlockSpec((None, None, blk, dm), lambda b, c: (0, b, c, 0))
    dir_b = pl.BlockSpec((None, None, blk, dm), lambda b, c: (1, b, back(c), 0))
    sh_f = pl.BlockSpec((None, blk, dm), lambda b, c: (b, c, 0))
    sh_b = pl.BlockSpec((None, blk, dm), lambda b, c: (b, back(c), 0))
    sds = jax.ShapeDtypeStruct((bsz, length, dm), F32)
    return pl.pallas_call(
        _wkv_kernel,
        grid=(bsz, nc),
        in_specs=[dir_f, dir_f, dir_f, sh_f, sh_f, sh_f, dir_b, dir_b, dir_b, sh_b, sh_b, sh_b],
        out_specs=[sh_f, sh_b],
        out_shape=[sds, sds],
        scratch_shapes=[pltpu.VMEM((2, dm // GL, GL, GL), F32)],
        compiler_params=_params(2),
        name="wkv_scan",
    )(logw, kdir, bvec, r, v, kk, logw, kdir, bvec, r, v, kk)


def _rwkv_readout_kernel(x_ref, mod_ref, yf_ref, yb_ref, bonus_ref, g_ref, lng_ref, lnb_ref, wo_ref, hsel_ref,
                         hselt_ref, ng2_ref, rw_ref, rb_ref, xo_ref, h2_ref, lg_ref):
    hsel, hselt = hsel_ref[...], hselt_ref[...]
    y = yf_ref[...] + yb_ref[...]
    yc = y - _head_sum(y, hsel, hselt) * (1.0 / HEAD)
    var = _head_sum(yc * yc, hsel, hselt) * (1.0 / HEAD)
    yn = yc * lax.rsqrt(var + GN_EPS) * lng_ref[...] + lnb_ref[...]
    z = (yn + bonus_ref[...]) * g_ref[...]
    xn = x_ref[...] + mod_ref[2:3, :] * _dot(z, wo_ref[...])
    xo_ref[...] = xn
    _moe_prologue(xn, mod_ref, ng2_ref, rw_ref, rb_ref, h2_ref, lg_ref)


def rwkv_readout(xs, mods, y_fwd, y_bwd, bonus, g, ln_g, ln_b, w_o, hsel, hselt, ng2, router_w, router_b):
    bsz, length, dm = xs.shape
    tile, mod = _tile_specs(dm, bsz)
    lg_tile = pl.BlockSpec((None, TOKEN_TILE, LANES), lambda b, t: (b, t, 0))
    vec = lambda a: a.reshape(1, -1)
    consts = [vec(ln_g), vec(ln_b), w_o.astype(BF16), hsel, hselt, vec(ng2), router_w, router_b]
    sds = jax.ShapeDtypeStruct
    return pl.pallas_call(
        _rwkv_readout_kernel,
        grid=(bsz, length // TOKEN_TILE),
        in_specs=[tile, mod, tile, tile, tile, tile] + [_const_spec(a.shape) for a in consts],
        out_specs=[tile, tile, lg_tile],
        out_shape=[sds((bsz, length, dm), F32), sds((bsz, length, dm), F32),
                   sds((bsz, length, LANES), F32)],
        compiler_params=_params(2),
        name="rwkv_readout",
    )(xs, mods, y_fwd, y_bwd, bonus, g, *consts)


def _conv_kernel(x_ref, *refs, n_ctx, n_ffn):
    ffn_refs, refs = refs[:n_ffn], refs[n_ffn:]
    mod_ref, ng_ref, win_ref, cw_ref, wout_ref, ng2_ref, rw_ref, rb_ref, xo_ref, h2_ref, lg_ref = refs
    x = x_ref[...]
    if n_ffn:
        x = _add_ffn(x, ffn_refs)
    dm = x.shape[1]
    h = _modulate(x, ng_ref[...], mod_ref[0:1, :], mod_ref[1:2, :])
    hw = _dot(h, win_ref[...])
    b_gate, c_gate, u = hw[:, :dm], hw[:, dm:2 * dm], hw[:, 2 * dm:]
    z = c_gate * u
    z_prev, z_next = _row_neighbours(z, _tile_row_len(n_ctx))
    z = cw_ref[0:1, :] * z_prev + cw_ref[1:2, :] * z + cw_ref[2:3, :] * z_next
    xn = x + mod_ref[2:3, :] * _dot(b_gate * z, wout_ref[...])
    xo_ref[...] = xn
    _moe_prologue(xn, mod_ref, ng2_ref, rw_ref, rb_ref, h2_ref, lg_ref)


def conv_mixer(xs, ffn, mods, ng1, w_in, conv_w, w_out, ng2, router_w, router_b, n_ctx):
    bsz, length, dm = xs.shape
    tile, mod = _tile_specs(dm, bsz)
    lg_tile = pl.BlockSpec((None, TOKEN_TILE, LANES), lambda b, t: (b, t, 0))
    vec = lambda a: a.reshape(1, -1)
    consts = [vec(ng1), w_in.astype(BF16), conv_w, w_out.astype(BF16), vec(ng2), router_w, router_b]
    sds = jax.ShapeDtypeStruct
    return pl.pallas_call(
        functools.partial(_conv_kernel, n_ctx=n_ctx, n_ffn=len(ffn)),
        grid=(bsz, length // TOKEN_TILE),
        in_specs=[tile] + (_ffn_specs(dm, bsz) if ffn else []) + [mod] + [_const_spec(a.shape) for a in consts],
        out_specs=[tile, tile, lg_tile],
        out_shape=[sds((bsz, length, dm), F32), sds((bsz, length, dm), F32),
                   sds((bsz, length, LANES), F32)],
        compiler_params=_params(2),
        name="conv_mixer",
    )(xs, *ffn, mods, *consts)


def _expert_kernel(be_ref, nu_ref, xb_ref, wgu_ref, bgu_ref, wd_ref, bdn_ref, o_ref, wgu_bf, wd_bf):
    i = pl.program_id(0)

    @pl.when(i < nu_ref[0])
    def _():
        new_expert = jnp.logical_or(i == 0, be_ref[i] != be_ref[jnp.maximum(i - 1, 0)])

        @pl.when(new_expert)
        def _():
            wgu_bf[...] = wgu_ref[...].astype(BF16)
            wd_bf[...] = wd_ref[...].astype(BF16)

        hgu = _dot(xb_ref[...], wgu_bf[...]) + bgu_ref[...]
        ff = hgu.shape[1] // 2
        g_ = jnp.minimum(hgu[:, :ff], SWIGLU_LIMIT)
        u_ = jnp.clip(hgu[:, ff:], -SWIGLU_LIMIT, SWIGLU_LIMIT)
        act = (u_ + 1.0) * (g_ * _sigmoid(SWIGLU_ALPHA * g_))
        o_ref[...] = _dot(act, wd_bf[...]) + bdn_ref[...]


def expert_ffn(block_e, n_used, xb, layer, w_gu, b_gu, w_down, b_down):
    n_rows, dm = xb.shape
    depth, n_e, _, ff2 = w_gu.shape
    grid_spec = pltpu.PrefetchScalarGridSpec(
        num_scalar_prefetch=2,
        grid=(n_rows // EXPERT_BLOCK,),
        in_specs=[pl.BlockSpec((EXPERT_BLOCK, dm), lambda i, be, nu: (i, 0)),
                  pl.BlockSpec((None, None, dm, ff2), lambda i, be, nu: (layer, be[i], 0, 0)),
                  pl.BlockSpec((None, None, 1, ff2), lambda i, be, nu: (layer, be[i], 0, 0)),
                  pl.BlockSpec((None, None, ff2 // 2, dm), lambda i, be, nu: (layer, be[i], 0, 0)),
                  pl.BlockSpec((None, None, 1, dm), lambda i, be, nu: (layer, be[i], 0, 0))],
        out_specs=pl.BlockSpec((EXPERT_BLOCK, dm), lambda i, be, nu: (i, 0)),
        scratch_shapes=[pltpu.VMEM((dm, ff2), BF16), pltpu.VMEM((ff2 // 2, dm), BF16)])
    return pl.pallas_call(
        _expert_kernel,
        grid_spec=grid_spec,
        out_shape=jax.ShapeDtypeStruct((n_rows, dm), F32),
        compiler_params=_params(1),
        name="expert_ffn",
    )(block_e, n_used, xb, w_gu, b_gu.reshape(depth, n_e, 1, ff2), w_down, b_down.reshape(depth, n_e, 1, dm))


def moe_ffn(h2, logits, layer, w_gu, b_gu, w_down, b_down):
    t, dm = h2.shape
    top_logit, top_e = lax.top_k(logits[:, :N_EXPERTS], TOP_K)
    gate = jax.nn.softmax(top_logit, axis=-1)
    expert = top_e.reshape(-1).astype(jnp.int32)
    n_assign = t * TOP_K
    eids = jnp.arange(N_EXPERTS, dtype=jnp.int32)
    onehot = (expert[:, None] == eids[None, :]).astype(jnp.int32)
    csum = jnp.cumsum(onehot, axis=0)
    counts = csum[-1]
    padded = (counts + EXPERT_BLOCK - 1) // EXPERT_BLOCK * EXPERT_BLOCK
    pad_end = jnp.cumsum(padded)
    pad_start = pad_end - padded
    dest = jnp.sum(onehot * (csum - 1 + pad_start[None, :]), axis=1)
    n_blocks = -(-(n_assign + N_EXPERTS * (EXPERT_BLOCK - 1)) // EXPERT_BLOCK)
    n_rows = n_blocks * EXPERT_BLOCK
    block_start = jnp.arange(n_blocks, dtype=jnp.int32) * EXPERT_BLOCK
    block_e = jnp.minimum(jnp.sum((pad_end[None, :] <= block_start[:, None]).astype(jnp.int32), axis=1),
                          N_EXPERTS - 1)
    token = jnp.arange(n_assign, dtype=jnp.int32) // TOP_K
    row_tok = jnp.zeros((n_rows,), jnp.int32).at[dest].add(token, unique_indices=True)
    n_used = (pad_end[-1:] // EXPERT_BLOCK).astype(jnp.int32)
    yb = expert_ffn(block_e, n_used, h2[row_tok], layer, w_gu, b_gu, w_down, b_down)
    dest_km = dest.reshape(t, TOP_K).T
    return [yb[dest_km[k]] for k in range(TOP_K)], gate


def _final_norm_kernel(x_ref, *refs):
    *ffn_refs, g_ref, o_ref = refs
    x = _add_ffn(x_ref[...], ffn_refs)
    ms = jnp.mean(x * x, axis=-1, keepdims=True)
    o_ref[...] = x * lax.rsqrt(ms + RMS_EPS) * g_ref[...]


def final_norm(xs, ffn, g, n_ctx):
    bsz, length, dm = xs.shape
    skip = n_ctx // TOKEN_TILE
    return pl.pallas_call(
        _final_norm_kernel,
        grid=(bsz, length // TOKEN_TILE - skip),
        in_specs=[pl.BlockSpec((None, TOKEN_TILE, dm), lambda b, t: (b, t + skip, 0))]
        + _ffn_specs(dm, bsz, skip) + [pl.BlockSpec((1, dm), lambda b, t: (0, 0))],
        out_specs=pl.BlockSpec((None, TOKEN_TILE, dm), lambda b, t: (b, t, 0)),
        out_shape=jax.ShapeDtypeStruct((bsz, length - n_ctx, dm), F32),
        compiler_params=_params(2),
        name="final_norm",
    )(xs, *ffn, g.reshape(1, dm))


def kernel(x, c, ctx, c_ctx, ada_w, ada_b, norm_g, final_norm_g, rw_mu, rw_w_rkv, rw_w0, rw_w1, rw_w2, rw_a0, rw_a1, rw_a2, rw_g1, rw_g2, rw_k_k, rw_k_a, rw_r_k, rw_ln_g, rw_ln_b, rw_w_o, sc_w_in, sc_conv, sc_w_out, moe_router_w, moe_router_b, moe_w_gu, moe_b_gu, moe_w_down, moe_b_down):
    bsz, seq, dm = x.shape
    n_ctx = ctx.shape[1]
    depth = ada_w.shape[0]
    assert n_ctx == TOKEN_TILE and seq % TOKEN_TILE == 0 and dm % GL == 0
    length = n_ctx + seq

    cond_rows = -(-(bsz + 1) // 8) * 8
    cond = jnp.concatenate([c, c_ctx[None], jnp.zeros((cond_rows - bsz - 1, dm), F32)], axis=0)
    mods = ada_mods(cond, ada_w, ada_b)
    xs = jnp.concatenate([ctx, x], axis=1)

    head_of_lane = jnp.arange(dm, dtype=jnp.int32) // HEAD
    hsel = (head_of_lane[:, None] == jnp.arange(LANES, dtype=jnp.int32)[None, :]).astype(BF16)
    hselt = hsel.T
    ffn = ()
    for i in range(depth):
        j = i // 2
        router_w = jnp.pad(moe_router_w[i], ((0, 0), (0, LANES - N_EXPERTS))).astype(BF16)
        router_b = jnp.pad(moe_router_b[i], (0, LANES - N_EXPERTS)).reshape(1, LANES)
        if i % 2 == 0:
            logw, kdir, bvec, r, v, kk, g, bonus, *x_new = rwkv_front(
                xs, ffn, mods[i], norm_g[i, 0], rw_mu[j], rw_w_rkv[j], rw_w0[j], rw_w1[j], rw_w2[j], rw_a0[j],
                rw_a1[j], rw_a2[j], rw_g1[j], rw_g2[j], rw_k_k[j], rw_k_a[j], rw_r_k[j], hsel, hselt, n_ctx)
            xs = x_new[0] if ffn else xs
            y_fwd, y_bwd = wkv_scan(logw, kdir, bvec, r, v, kk, n_ctx)
            xs, h2, logits = rwkv_readout(xs, mods[i], y_fwd, y_bwd, bonus, g, rw_ln_g[j], rw_ln_b[j], rw_w_o[j],
                                          hsel, hselt, norm_g[i, 1], router_w, router_b)
        else:
            xs, h2, logits = conv_mixer(xs, ffn, mods[i], norm_g[i, 0], sc_w_in[j], sc_conv[j], sc_w_out[j],
                                        norm_g[i, 1], router_w, router_b, n_ctx)
        rows, gate = moe_ffn(h2.reshape(bsz * length, dm), logits.reshape(bsz * length, LANES),
                             i, moe_w_gu, moe_b_gu, moe_w_down, moe_b_down)
        ffn = tuple(a.reshape(bsz, length, dm) for a in rows) + (gate.reshape(bsz, length, TOP_K), mods[i])
    return final_norm(xs, ffn, final_norm_g, n_ctx)
```
